```python
import math
import jax, jax.numpy as jnp
from jax import lax
import numpy as np


D_MODEL = 2048
BATCH = 4
SEQ = 8192
DEPTH = 2

HEAD_DIM = 128
MIX_WIDTH = D_MODEL
WA = MIX_WIDTH // 2
WB = MIX_WIDTH - WA
HA = WA // HEAD_DIM
HB = WB // HEAD_DIM
WC = MIX_WIDTH // 2
WD = MIX_WIDTH - WC
NC = WC // HEAD_DIM
BLOCK_C = HEAD_DIM
HD = WD // HEAD_DIM
CONV_WIDTH = 4
CHUNK = 64
RG_C = 8.0
ROPE_BASE = 10000.0
D_FF = ((8 * D_MODEL // 3 + 255) // 256) * 256
N_EXPERTS = 8
TOP_K = 2
MOE_BLOCK = 512
EPS = 1e-6
N_EVEN = (DEPTH + 1) // 2
N_ODD = DEPTH // 2
SPLIT_AB = [3 * WA, 4 * WA, 4 * WA + HA, 4 * WA + 2 * HA, 4 * WA + 2 * HA + WB, 4 * WA + 2 * HA + 2 * WB, 4 * WA + 2 * HA + 3 * WB]
COLS_AB = 4 * WA + 2 * HA + 4 * WB
SPLIT_CD = [WC, 2 * WC, 2 * WC + WD, 2 * WC + 2 * WD, 2 * WC + 3 * WD]
COLS_CD = 2 * WC + 4 * WD

kernel_name = "hybrid_deltanet_retention_rglru_hgrn2_moe"


def rms_norm(x, g):
    xf = x.astype(jnp.float32)
    y = xf * lax.rsqrt(jnp.mean(xf * xf, axis=-1, keepdims=True) + EPS)
    return (y * g.astype(jnp.float32)).astype(x.dtype)


def l2norm(t):
    return t * lax.rsqrt(jnp.sum(t * t, axis=-1, keepdims=True) + EPS)


def swiglu(x, w_gate, w_up, w_down):
    return (jax.nn.silu(x @ w_gate) * (x @ w_up)) @ w_down


def to_heads(t, n_heads):
    b, s, _ = t.shape
    return t.reshape(b, s, n_heads, -1).transpose(0, 2, 1, 3)


def from_heads(t):
    return t.transpose(0, 2, 1, 3)


def to_chunks(t):
    b, h, s = t.shape[:3]
    return jnp.moveaxis(t.reshape(b, h, s // CHUNK, CHUNK, *t.shape[3:]), 2, 0)


def from_chunks(t):
    n, b, h, c = t.shape[:4]
    return jnp.moveaxis(t, 0, 2).reshape(b, h, n * c, *t.shape[4:])


def causal_conv(x, w):
    k, c = w.shape
    return lax.conv_general_dilated(x, w[:, None, :], window_strides=(1,), padding=[(k - 1, 0)],
                                    dimension_numbers=('NWC', 'WIO', 'NWC'), feature_group_count=c)


def apply_rotary(t, cos, sin):
    half = t.shape[-1] // 2
    t1, t2 = t[..., :half], t[..., half:]
    return jnp.concatenate([t1 * cos - t2 * sin, t1 * sin + t2 * cos], axis=-1)


def gated_delta_rule(q, k, v, log_decay, beta):
    bsz, h, s, dk = q.shape
    dv = v.shape[-1]
    causal = jnp.tril(jnp.ones((CHUNK, CHUNK), bool))
    strict = jnp.tril(jnp.ones((CHUNK, CHUNK), bool), -1)
    eye = jnp.eye(CHUNK, dtype=jnp.float32)

    def step(state, inp):
        qc, kc, vc, gc, bc = inp
        gam = jnp.cumsum(gc, axis=-1)
        diff = gam[..., :, None] - gam[..., None, :]
        dec = jnp.where(causal, jnp.exp(jnp.where(causal, diff, 0.0)), 0.0)
        kb = kc * bc[..., None]
        lower = jnp.where(strict, jnp.einsum('bhrd,bhsd->bhrs', kb, kc) * dec, 0.0) + eye
        rhs = jnp.concatenate([vc * bc[..., None], kb * jnp.exp(gam)[..., None]], axis=-1)
        sol = lax.linalg.triangular_solve(lower, rhs, left_side=True, lower=True, unit_diagonal=True)
        u, w = sol[..., :dv], sol[..., dv:]
        v_new = u - jnp.einsum('bhrd,bhde->bhre', w, state)
        scores = jnp.einsum('bhrd,bhsd->bhrs', qc, kc) * dec
        o = (jnp.einsum('bhrd,bhde->bhre', qc * jnp.exp(gam)[..., None], state)
             + jnp.einsum('bhrs,bhse->bhre', scores, v_new))
        g_last = gam[..., -1:]
        state = (jnp.exp(g_last)[..., None] * state
                 + jnp.einsum('bhsd,bhse->bhde', kc * jnp.exp(g_last - gam)[..., None], v_new))
        return state, o

    s0 = jnp.zeros((bsz, h, dk, dv), jnp.float32)
    xs = (to_chunks(q), to_chunks(k), to_chunks(v), to_chunks(log_decay), to_chunks(beta))
    _, o = lax.scan(step, s0, xs)
    return from_chunks(o)


def retention(q, k, v):
    bsz, h, s, dk = q.shape
    dv = v.shape[-1]
    log_gamma = jnp.log1p(-jnp.exp2(-5.0 - jnp.arange(h, dtype=jnp.float32)))
    idx = jnp.arange(CHUNK, dtype=jnp.float32)
    rel = idx[:, None] - idx[None, :]
    dmat = jnp.where(rel >= 0, jnp.exp(jnp.maximum(rel, 0.0)[None] * log_gamma[:, None, None]), 0.0)
    xi = jnp.exp((idx + 1.0)[None, :] * log_gamma[:, None])[..., None]
    zeta = jnp.exp((CHUNK - 1.0 - idx)[None, :] * log_gamma[:, None])[..., None]
    g_chunk = jnp.exp(CHUNK * log_gamma)[:, None, None]

    def step(r, inp):
        qc, kc, vc = inp
        scores = jnp.einsum('bhrd,bhsd->bhrs', qc, kc) * dmat
        o = jnp.einsum('bhrs,bhse->bhre', scores, vc) + jnp.einsum('bhrd,bhde->bhre', qc * xi, r)
        r = g_chunk * r + jnp.einsum('bhsd,bhse->bhde', kc * zeta, vc)
        return r, o

    r0 = jnp.zeros((bsz, h, dk, dv), jnp.float32)
    _, o = lax.scan(step, r0, (to_chunks(q), to_chunks(k), to_chunks(v)))
    return from_chunks(o)


def linear_recurrence(a, b):
    def combine(x, y):
        return (x[0] * y[0], y[0] * x[1] + y[1])
    return lax.associative_scan(combine, (a, b), axis=1)[1]


def hgrn2_recurrence(q, k, v, log_f):
    bsz, h, s, dk = q.shape
    dv = v.shape[-1]
    causal = jnp.tril(jnp.ones((CHUNK, CHUNK), bool))[:, :, None]

    def step(state, inp):
        qc, kc, vc, lc = inp
        gam = jnp.cumsum(lc, axis=2)
        diff = gam[:, :, :, None, :] - gam[:, :, None, :, :]
        dec = jnp.where(causal, jnp.exp(jnp.where(causal, diff, 0.0)), 0.0)
        scores = jnp.einsum('bhrd,bhsd,bhrsd->bhrs', qc, kc, dec)
        o = (jnp.einsum('bhrd,bhde->bhre', qc * jnp.exp(gam), state)
             + jnp.einsum('bhrs,bhse->bhre', scores, vc))
        g_last = gam[:, :, -1:, :]
        state = (jnp.exp(g_last[:, :, 0, :])[..., None] * state
                 + jnp.einsum('bhsd,bhse->bhde', kc * jnp.exp(g_last - gam), vc))
        return state, o

    s0 = jnp.zeros((bsz, h, dk, dv), jnp.float32)
    _, o = lax.scan(step, s0, (to_chunks(q), to_chunks(k), to_chunks(v), to_chunks(log_f)))
    return from_chunks(o)


def mixer_ab(h, w_in, conv_w, a_log, dt_bias, norm_a, norm_b, w_out):
    f32 = jnp.float32
    bsz, s, _ = h.shape
    proj = (h @ w_in).astype(f32)
    qkv_a, z_a, beta_a, alpha_a, q_b, k_b, v_b, g_b = jnp.split(proj, SPLIT_AB, axis=-1)
    qkv_a = jax.nn.silu(causal_conv(qkv_a, conv_w.astype(f32)))
    q_a, k_a, v_a = jnp.split(qkv_a, 3, axis=-1)
    q_a = l2norm(to_heads(q_a, HA)) * HEAD_DIM ** -0.5
    k_a = l2norm(to_heads(k_a, HA))
    v_a = to_heads(v_a, HA)
    beta = jax.nn.sigmoid(beta_a).transpose(0, 2, 1)
    log_decay = (-jnp.exp(a_log.astype(f32)) * jax.nn.softplus(alpha_a + dt_bias.astype(f32))).transpose(0, 2, 1)
    o_a = from_heads(gated_delta_rule(q_a, k_a, v_a, log_decay, beta))
    o_a = (rms_norm(o_a, norm_a) * jax.nn.silu(z_a.reshape(bsz, s, HA, HEAD_DIM))).reshape(bsz, s, WA)
    pos = jnp.arange(s, dtype=f32)
    theta = 1.0 / ROPE_BASE ** jnp.linspace(0.0, 1.0, HEAD_DIM // 2, dtype=f32)
    ang = pos[:, None] * theta[None, :]
    cos, sin = jnp.cos(ang), jnp.sin(ang)
    q_r = apply_rotary(to_heads(q_b, HB), cos, sin)
    k_r = apply_rotary(to_heads(k_b, HB), cos, sin) * HEAD_DIM ** -0.5
    o_b = from_heads(retention(q_r, k_r, to_heads(v_b, HB)))
    mu = jnp.mean(o_b, axis=-1, keepdims=True)
    var = jnp.mean(jnp.square(o_b - mu), axis=-1, keepdims=True)
    o_b = ((o_b - mu) * lax.rsqrt(var + EPS)).reshape(bsz, s, WB) * norm_b.astype(f32) * jax.nn.silu(g_b)
    return jnp.concatenate([o_a, o_b], axis=-1).astype(h.dtype) @ w_out


def mixer_cd(h, w_in, conv_w, conv_b, rg_wa, rg_ba, rg_wx, rg_bx, rg_lambda, lower_bound, norm_d, w_out):
    f32 = jnp.float32
    bsz, s, _ = h.shape
    proj = (h @ w_in).astype(f32)
    y_c, x_c, q_d, f_d, i_d, g_d = jnp.split(proj, SPLIT_CD, axis=-1)
    u = causal_conv(x_c, conv_w.astype(f32)) + conv_b.astype(f32)
    ug = u.reshape(bsz, s, NC, BLOCK_C)
    r_gate = jax.nn.sigmoid(jnp.einsum('bsgi,gij->bsgj', ug, rg_wa.astype(f32)) + rg_ba.astype(f32).reshape(NC, BLOCK_C))
    i_gate = jax.nn.sigmoid(jnp.einsum('bsgi,gij->bsgj', ug, rg_wx.astype(f32)) + rg_bx.astype(f32).reshape(NC, BLOCK_C))
    log_a = -RG_C * r_gate * jax.nn.softplus(-rg_lambda.astype(f32).reshape(NC, BLOCK_C))
    x_in = jnp.sqrt(-jnp.expm1(2.0 * log_a)) * (i_gate * ug)
    h_c = linear_recurrence(jnp.exp(log_a), x_in).reshape(bsz, s, WC)
    o_c = jax.nn.gelu(y_c, approximate=True) * h_c
    lb = lower_bound.astype(f32).reshape(HD, 1, HEAD_DIM)
    f_lin = to_heads(f_d, HD)
    log_f = jnp.logaddexp(jnp.log(lb), jnp.log1p(-lb) + jax.nn.log_sigmoid(f_lin))
    k_d = (1.0 - lb) * jax.nn.sigmoid(-f_lin)
    q_h = jax.nn.silu(to_heads(q_d, HD))
    o_d = from_heads(hgrn2_recurrence(q_h, k_d, to_heads(i_d, HD), log_f))
    o_d = (rms_norm(o_d, norm_d) * jax.nn.silu(g_d.reshape(bsz, s, HD, HEAD_DIM))).reshape(bsz, s, WD)
    return jnp.concatenate([o_c, o_d], axis=-1).astype(h.dtype) @ w_out


def moe_swiglu(h, w_router, w_gate, w_up, w_down):
    bsz, s, d = h.shape
    t = bsz * s
    hf = h.reshape(t, d)
    logits = (hf @ w_router).astype(jnp.float32)
    top_logits, top_idx = lax.top_k(logits, TOP_K)
    gates = jax.nn.softmax(top_logits, axis=-1).reshape(-1)
    flat_e = top_idx.reshape(-1)
    flat_t = jnp.repeat(jnp.arange(t, dtype=jnp.int32), TOP_K)
    order = jnp.argsort(flat_e)
    e_s, t_s, g_s = flat_e[order], flat_t[order], gates[order]
    counts = jnp.bincount(flat_e, length=N_EXPERTS)
    padded = (counts + MOE_BLOCK - 1) // MOE_BLOCK * MOE_BLOCK
    p_end = jnp.cumsum(padded)
    rank = jnp.arange(t * TOP_K, dtype=jnp.int32) - (jnp.cumsum(counts) - counts)[e_s]
    dest = (p_end - padded)[e_s] + rank
    n_blocks = -(-(t * TOP_K) // MOE_BLOCK) + N_EXPERTS
    x_pad = jnp.zeros((n_blocks * MOE_BLOCK, d), h.dtype).at[dest].set(hf[t_s])
    block_e = jnp.minimum(jnp.searchsorted(p_end, jnp.arange(n_blocks, dtype=jnp.int32) * MOE_BLOCK, side='right'), N_EXPERTS - 1)

    def expert_block(args):
        xb, e = args
        return swiglu(xb, w_gate[e], w_up[e], w_down[e])

    y_pad = lax.map(expert_block, (x_pad.reshape(n_blocks, MOE_BLOCK, d), block_e)).reshape(-1, d)
    y = y_pad[dest] * g_s[:, None].astype(h.dtype)
    return jnp.zeros((t, d), h.dtype).at[t_s].add(y).reshape(bsz, s, d)


def setup_inputs(seed: int = 0) -> dict:
    key = jax.random.key(seed)
    ks = jax.random.split(key, 32)
    f32 = jnp.float32

    def nrm(k, shape, fan_in):
        return jax.random.normal(k, shape, f32) * fan_in ** -0.5

    def gain(k, shape):
        return 1.0 + 0.02 * jax.random.normal(k, shape, f32)

    dt = jnp.exp(jax.random.uniform(ks[6], (N_EVEN, HA), f32, math.log(1e-3), math.log(1e-1)))
    a0 = jax.random.uniform(ks[20], (N_ODD, WC), f32, 0.9, 0.999)
    return {
        'x': jax.random.normal(ks[0], (BATCH, SEQ, D_MODEL), f32),
        'norm_mix': gain(ks[1], (DEPTH, D_MODEL)),
        'norm_ffn': gain(ks[2], (DEPTH, D_MODEL)),
        'w_in_ab': nrm(ks[3], (N_EVEN, D_MODEL, COLS_AB), D_MODEL),
        'conv_a': nrm(ks[4], (N_EVEN, CONV_WIDTH, 3 * WA), CONV_WIDTH),
        'a_log': jnp.log(jax.random.uniform(ks[5], (N_EVEN, HA), f32, 1.0, 16.0)),
        'dt_bias': dt + jnp.log(-jnp.expm1(-dt)),
        'norm_a': gain(ks[7], (N_EVEN, HEAD_DIM)),
        'norm_b': gain(ks[8], (N_EVEN, WB)),
        'w_out_ab': nrm(ks[9], (N_EVEN, WA + WB, D_MODEL), WA + WB),
        'w_gate_dense': nrm(ks[10], (N_EVEN, D_MODEL, D_FF), D_MODEL),
        'w_up_dense': nrm(ks[11], (N_EVEN, D_MODEL, D_FF), D_MODEL),
        'w_down_dense': nrm(ks[12], (N_EVEN, D_FF, D_MODEL), D_FF),
        'w_in_cd': nrm(ks[13], (N_ODD, D_MODEL, COLS_CD), D_MODEL),
        'conv_c_w': nrm(ks[14], (N_ODD, CONV_WIDTH, WC), CONV_WIDTH),
        'conv_c_b': 0.02 * jax.random.normal(ks[15], (N_ODD, WC), f32),
        'rg_wa': nrm(ks[16], (N_ODD, NC, BLOCK_C, BLOCK_C), BLOCK_C),
        'rg_ba': 0.02 * jax.random.normal(ks[17], (N_ODD, WC), f32),
        'rg_wx': nrm(ks[18], (N_ODD, NC, BLOCK_C, BLOCK_C), BLOCK_C),
        'rg_bx': 0.02 * jax.random.normal(ks[19], (N_ODD, WC), f32),
        'rg_lambda': jnp.log(a0) - jnp.log1p(-a0),
        'hgrn_lb': 1.0 + 0.1 * jax.random.normal(ks[21], (DEPTH, WD), f32),
        'norm_d': gain(ks[22], (N_ODD, HEAD_DIM)),
        'w_out_cd': nrm(ks[23], (N_ODD, WC + WD, D_MODEL), WC + WD),
        'w_router': nrm(ks[24], (N_ODD, D_MODEL, N_EXPERTS), D_MODEL),
        'w_gate_moe': nrm(ks[25], (N_ODD, N_EXPERTS, D_MODEL, D_FF), D_MODEL),
        'w_up_moe': nrm(ks[26], (N_ODD, N_EXPERTS, D_MODEL, D_FF), D_MODEL),
        'w_down_moe': nrm(ks[27], (N_ODD, N_EXPERTS, D_FF, D_MODEL), D_FF),
        'norm_final': gain(ks[28], (D_MODEL,)),
    }


def reference(x, norm_mix, norm_ffn, w_in_ab, conv_a, a_log, dt_bias, norm_a, norm_b, w_out_ab,
              w_gate_dense, w_up_dense, w_down_dense, w_in_cd, conv_c_w, conv_c_b, rg_wa, rg_ba,
              rg_wx, rg_bx, rg_lambda, hgrn_lb, norm_d, w_out_cd, w_router, w_gate_moe, w_up_moe,
              w_down_moe, norm_final):
    lb_soft = jax.nn.softmax(hgrn_lb.astype(jnp.float32), axis=0)
    lb_all = jnp.cumsum(lb_soft, axis=0) - lb_soft[0:1]
    for layer in range(DEPTH):
        j = layer // 2
        hn = rms_norm(x, norm_mix[layer])
        if layer % 2 == 0:
            x = x + mixer_ab(hn, w_in_ab[j], conv_a[j], a_log[j], dt_bias[j], norm_a[j], norm_b[j], w_out_ab[j])
            x = x + swiglu(rms_norm(x, norm_ffn[layer]), w_gate_dense[j], w_up_dense[j], w_down_dense[j])
        else:
            x = x + mixer_cd(hn, w_in_cd[j], conv_c_w[j], conv_c_b[j], rg_wa[j], rg_ba[j], rg_wx[j], rg_bx[j],
                             rg_lambda[j], lb_all[layer], norm_d[j], w_out_cd[j])
            x = x + moe_swiglu(rms_norm(x, norm_ffn[layer]), w_router[j], w_gate_moe[j], w_up_moe[j], w_down_moe[j])
    return rms_norm(x, norm_final)
```

```python
import functools
import math

import jax
import jax.numpy as jnp
from jax import lax
from jax.experimental import pallas as pl
from jax.experimental.pallas import tpu as pltpu

F32 = jnp.float32
BF16 = jnp.bfloat16
LANES = 128
SUBLANES = 8
EPS = 1e-6
CONV_K = 4
RG_C = 8.0
ROPE_BASE = 10000.0
N_EXPERTS = 8
TOP_K = 2
CHUNK = 128
VMEM_LIMIT = 48 * 2**20


def _params(sem):
    return pltpu.CompilerParams(dimension_semantics=sem, vmem_limit_bytes=VMEM_LIMIT)


def _dot(a, b):
    return jnp.dot(a, b, preferred_element_type=F32)


def _dot_nt(a, b):
    return lax.dot_general(a, b, (((1,), (1,)), ((), ())), preferred_element_type=F32)


def _dot_tn(a, b):
    return lax.dot_general(a, b, (((0,), (0,)), ((), ())), preferred_element_type=F32)


def _softplus(x):
    return jnp.maximum(x, 0.0) + jnp.log1p(jnp.exp(-jnp.abs(x)))


def _silu(x):
    return x * jax.nn.sigmoid(x)


def _cumsum_rows(x):
    n = x.shape[0]
    rows = lax.broadcasted_iota(jnp.int32, x.shape, 0)
    d = 1
    while d < n:
        x = x + jnp.where(rows >= d, pltpu.roll(x, d, 0), 0.0)
        d *= 2
    return x


def _rms_kernel(x_ref, g_ref, o_ref):
    x = x_ref[...]
    ms = jnp.mean(x * x, axis=-1, keepdims=True)
    o_ref[...] = (x * lax.rsqrt(ms + EPS) * g_ref[...]).astype(o_ref.dtype)


def _rms_cast(x, g, tm=512):
    t, d = x.shape
    out_dtype = BF16
    return pl.pallas_call(
        _rms_kernel,
        grid=(t // tm,),
        in_specs=[pl.BlockSpec((tm, d), lambda i: (i, 0)), pl.BlockSpec((1, d), lambda i: (0, 0))],
        out_specs=pl.BlockSpec((tm, d), lambda i: (i, 0)),
        out_shape=jax.ShapeDtypeStruct((t, d), out_dtype),
        compiler_params=_params(("parallel",)),
        name="rms_cast",
    )(x, g.reshape(1, d))


def _mm_kernel(*refs, n_a, has_res):
    o_ref = refs[-1]
    acc = _dot(refs[0][...], refs[n_a][...])
    for t in range(1, n_a):
        acc = acc + _dot(refs[t][...], refs[n_a + t][...])
    if has_res:
        acc = acc + refs[2 * n_a][...]
    o_ref[...] = acc.astype(o_ref.dtype)


def _matmul(a_list, w_list, res=None, out_dtype=F32, tm=1024, tn=512, name="matmul"):
    m = a_list[0].shape[0]
    n = w_list[0].shape[1]
    tm, tn = min(tm, m), min(tn, n)
    in_specs = [pl.BlockSpec((tm, a.shape[1]), lambda i, j: (i, 0)) for a in a_list]
    in_specs += [pl.BlockSpec((w.shape[0], tn), lambda i, j: (0, j)) for w in w_list]
    args = list(a_list) + list(w_list)
    if res is not None:
        in_specs.append(pl.BlockSpec((tm, tn), lambda i, j: (i, j)))
        args.append(res)
    return pl.pallas_call(
        functools.partial(_mm_kernel, n_a=len(a_list), has_res=res is not None),
        grid=(m // tm, n // tn),
        in_specs=in_specs,
        out_specs=pl.BlockSpec((tm, tn), lambda i, j: (i, j)),
        out_shape=jax.ShapeDtypeStruct((m, n), out_dtype),
        compiler_params=_params(("parallel", "arbitrary")),
        name=name,
    )(*args)


def _ffn_kernel(be_ref, nv_ref, x_ref, wg_ref, wu_ref, wd_ref, *rest, has_res):
    o_ref = rest[-1]
    i, j = pl.program_id(0), pl.program_id(1)
    valid = i < nv_ref[0]

    @pl.when(valid)
    def _():
        x = x_ref[...].astype(BF16)
        a = _silu(_dot(x, wg_ref[0])) * _dot(x, wu_ref[0])
        p = _dot(a.astype(BF16), wd_ref[0])

        @pl.when(j == 0)
        def _():
            o_ref[...] = p + rest[0][...] if has_res else p

        @pl.when(j > 0)
        def _():
            o_ref[...] += p

    @pl.when(jnp.logical_and(jnp.logical_not(valid), j == 0))
    def _():
        o_ref[...] = jnp.zeros_like(o_ref)


def _ffn(x, wg, wu, wd, block_e, n_valid, res=None, tm=512, tf=512):
    m, d = x.shape
    f = wg.shape[2]
    nb, nf = m // tm, f // tf

    def wcol(i, j, be, nv):
        return (be[i], 0, jnp.where(i < nv[0], j, nf - 1))

    def wrow(i, j, be, nv):
        return (be[i], jnp.where(i < nv[0], j, nf - 1), 0)

    in_specs = [
        pl.BlockSpec((tm, d), lambda i, j, be, nv: (i, 0)),
        pl.BlockSpec((1, d, tf), wcol),
        pl.BlockSpec((1, d, tf), wcol),
        pl.BlockSpec((1, tf, d), wrow),
    ]
    args = [x, wg, wu, wd]
    if res is not None:
        in_specs.append(pl.BlockSpec((tm, d), lambda i, j, be, nv: (i, 0)))
        args.append(res)
    return pl.pallas_call(
        functools.partial(_ffn_kernel, has_res=res is not None),
        grid_spec=pltpu.PrefetchScalarGridSpec(
            num_scalar_prefetch=2,
            grid=(nb, nf),
            in_specs=in_specs,
            out_specs=pl.BlockSpec((tm, d), lambda i, j, be, nv: (i, 0)),
        ),
        out_shape=jax.ShapeDtypeStruct((m, d), F32),
        compiler_params=_params(("parallel", "arbitrary")),
        name="ffn_swiglu",
    )(block_e, n_valid, *args)


def _conv_silu(tail_ref, x_ref, cw_ref, r0, lo, bias=None, act=True):
    sl = slice(lo, lo + LANES)
    win = jnp.concatenate([tail_ref[:, sl], x_ref[pl.ds(r0, CHUNK), sl]], axis=0)
    w = cw_ref[:, sl]
    y = win * w[CONV_K - 1:CONV_K]
    for k in range(1, CONV_K):
        y = y + pltpu.roll(win, k, 0) * w[CONV_K - 1 - k:CONV_K - k]
    y = y[SUBLANES:]
    if bias is not None:
        y = y + bias
    return _silu(y) if act else y


def _rms_gate(o, gain, z):
    ms = jnp.mean(o * o, axis=-1, keepdims=True)
    return o * lax.rsqrt(ms + EPS) * gain * _silu(z)


def _gdn_kernel(q_ref, k_ref, v_ref, z_ref, beta_ref, alpha_ref, cw_ref, alog_ref, dtb_ref, na_ref,
                o_ref, s_ref, tq_ref, tk_ref, tv_ref, *, n_heads):
    C = CHUNK
    tb = q_ref.shape[0]

    @pl.when(pl.program_id(1) == 0)
    def _():
        s_ref[...] = jnp.zeros_like(s_ref)
        tq_ref[...] = jnp.zeros_like(tq_ref)
        tk_ref[...] = jnp.zeros_like(tk_ref)
        tv_ref[...] = jnp.zeros_like(tv_ref)

    row = lax.broadcasted_iota(jnp.int32, (C, C), 0)
    col = lax.broadcasted_iota(jnp.int32, (C, C), 1)
    causal = row >= col
    strict = row > col
    eye = (row == col).astype(F32)
    neg_a = -jnp.exp(alog_ref[...])
    dtb = dtb_ref[...]
    gain = na_ref[...]
    scale = float(LANES) ** -0.5

    def chunk(c, carry):
        r0 = pl.multiple_of(c * C, C)
        beta_c = jax.nn.sigmoid(beta_ref[pl.ds(r0, C), :])
        g_c = neg_a * _softplus(alpha_ref[pl.ds(r0, C), :] + dtb)
        gam_c = _cumsum_rows(g_c)
        gam_t = gam_c.T
        for h in range(n_heads):
            lo = h * LANES
            q = _conv_silu(tq_ref, q_ref, cw_ref.at[0], r0, lo)
            k = _conv_silu(tk_ref, k_ref, cw_ref.at[1], r0, lo)
            v = _conv_silu(tv_ref, v_ref, cw_ref.at[2], r0, lo)
            q = q * lax.rsqrt(jnp.sum(q * q, axis=-1, keepdims=True) + EPS) * scale
            k = k * lax.rsqrt(jnp.sum(k * k, axis=-1, keepdims=True) + EPS)
            gcol = gam_c[:, h:h + 1]
            grow = gam_t[h:h + 1, :]
            bcol = beta_c[:, h:h + 1]
            dec = jnp.where(causal, jnp.exp(jnp.where(causal, gcol - grow, 0.0)), 0.0)
            kb = k * bcol
            kbf = k.astype(BF16)
            m1 = _dot_nt(jnp.concatenate([kb, q], axis=0).astype(BF16), kbf)
            low = jnp.where(strict, m1[:C] * dec, 0.0)
            att = m1[C:] * dec
            tinv = eye - low
            lp = low.astype(BF16)
            n_sq = int(math.log2(C)) - 1
            for it in range(n_sq):
                lp32 = _dot(lp, lp)
                lp = lp32.astype(BF16)
                tinv = tinv + _dot(tinv.astype(BF16), lp)
            egam = jnp.exp(gcol)
            rhs = jnp.concatenate([v * bcol, kb * egam], axis=1).astype(BF16)
            sol = _dot(tinv.astype(BF16), rhs)
            u, w = sol[:, :LANES], sol[:, LANES:]
            glast = gam_c[C - 1:C, h:h + 1]
            s_h = s_ref[h]
            wq = _dot(jnp.concatenate([w, q * egam], axis=0).astype(BF16), s_h.astype(BF16))
            v_new = u - wq[:C]
            vnb = v_new.astype(BF16)
            o = wq[C:] + _dot(att.astype(BF16), vnb)
            kg = (k * jnp.exp(glast - gcol)).astype(BF16)
            s_ref[h] = jnp.exp(glast) * s_h + _dot_tn(kg, vnb)
            z = z_ref[pl.ds(r0, C), lo:lo + LANES]
            o_ref[pl.ds(r0, C), lo:lo + LANES] = _rms_gate(o, gain, z).astype(o_ref.dtype)
        tq_ref[...] = q_ref[pl.ds(r0 + C - SUBLANES, SUBLANES), :]
        tk_ref[...] = k_ref[pl.ds(r0 + C - SUBLANES, SUBLANES), :]
        tv_ref[...] = v_ref[pl.ds(r0 + C - SUBLANES, SUBLANES), :]
        return carry

    lax.fori_loop(0, tb // C, chunk, 0)


def _gdn(proj, conv_w, a_log, dt_bias, norm_a, bsz, seq, wa, tb=256):
    n_heads = wa // LANES
    nt = seq // tb
    gate_blk = 8 * wa // LANES

    def colblk(cb):
        return pl.BlockSpec((tb, wa), lambda b, t: (b * nt + t, cb))

    def pad_row(p):
        return jnp.zeros((1, LANES), F32).at[0, :n_heads].set(p.astype(F32))

    small = lambda shape: pl.BlockSpec(shape, lambda b, t: (0,) * len(shape))
    return pl.pallas_call(
        functools.partial(_gdn_kernel, n_heads=n_heads),
        grid=(bsz, nt),
        in_specs=[colblk(0), colblk(1), colblk(2), colblk(3),
                  pl.BlockSpec((tb, LANES), lambda b, t: (b * nt + t, gate_blk)),
                  pl.BlockSpec((tb, LANES), lambda b, t: (b * nt + t, gate_blk + 1)),
                  small((3, CONV_K, wa)), small((1, LANES)), small((1, LANES)), small((1, LANES))],
        out_specs=pl.BlockSpec((tb, wa), lambda b, t: (b * nt + t, 0)),
        out_shape=jax.ShapeDtypeStruct((bsz * seq, wa), BF16),
        scratch_shapes=[pltpu.VMEM((n_heads, LANES, LANES), F32),
                        pltpu.VMEM((SUBLANES, wa), F32), pltpu.VMEM((SUBLANES, wa), F32),
                        pltpu.VMEM((SUBLANES, wa), F32)],
        compiler_params=_params(("parallel", "arbitrary")),
        name="gdn",
    )(proj, proj, proj, proj, proj, proj,
      conv_w.astype(F32).reshape(CONV_K, 3, wa).transpose(1, 0, 2),
      pad_row(a_log), pad_row(dt_bias), norm_a.astype(F32).reshape(1, LANES))


def _ret_kernel(q_ref, k_ref, v_ref, g_ref, cos_ref, sin_ref, nb_ref, o_ref, r_ref, dm_ref, xi_ref, zeta_ref,
                *, n_heads):
    C = CHUNK
    tb = q_ref.shape[0]
    log_gammas = [math.log1p(-(2.0 ** (-5.0 - h))) for h in range(n_heads)]

    @pl.when(pl.program_id(1) == 0)
    def _():
        r_ref[...] = jnp.zeros_like(r_ref)
        row = lax.broadcasted_iota(jnp.int32, (C, C), 0)
        col = lax.broadcasted_iota(jnp.int32, (C, C), 1)
        rel = (row - col).astype(F32)
        idx = lax.broadcasted_iota(jnp.int32, (C, LANES), 0).astype(F32)
        for h in range(n_heads):
            dm_ref[h] = jnp.where(rel >= 0, jnp.exp(jnp.maximum(rel, 0.0) * log_gammas[h]), 0.0)
            xi_ref[h] = jnp.exp((idx + 1.0) * log_gammas[h])
            zeta_ref[h] = jnp.exp((C - 1.0 - idx) * log_gammas[h])

    scale = float(LANES) ** -0.5

    def chunk(c, carry):
        r0 = pl.multiple_of(c * C, C)
        cos = cos_ref[pl.ds(r0, C), :]
        sin = sin_ref[pl.ds(r0, C), :]
        for h in range(n_heads):
            lo = h * LANES
            sl = slice(lo, lo + LANES)
            g_chunk = math.exp(C * log_gammas[h])
            q = q_ref[pl.ds(r0, C), sl]
            k = k_ref[pl.ds(r0, C), sl]
            v = v_ref[pl.ds(r0, C), sl]
            q = q * cos + pltpu.roll(q, LANES // 2, 1) * sin
            k = (k * cos + pltpu.roll(k, LANES // 2, 1) * sin) * scale
            kb = k.astype(BF16)
            vb = v.astype(BF16)
            scores = _dot_nt(q.astype(BF16), kb) * dm_ref[h]
            r_h = r_ref[h]
            o = _dot(scores.astype(BF16), vb) + _dot((q * xi_ref[h]).astype(BF16), r_h.astype(BF16))
            r_ref[h] = g_chunk * r_h + _dot_tn((k * zeta_ref[h]).astype(BF16), vb)
            mu = jnp.mean(o, axis=-1, keepdims=True)
            oc = o - mu
            var = jnp.mean(oc * oc, axis=-1, keepdims=True)
            g = g_ref[pl.ds(r0, C), sl]
            o_ref[pl.ds(r0, C), sl] = (oc * lax.rsqrt(var + EPS) * nb_ref[:, sl] * _silu(g)).astype(o_ref.dtype)
        return carry

    lax.fori_loop(0, tb // C, chunk, 0)


def _retention(proj, norm_b, bsz, seq, wb, col0, tb=256):
    n_heads = wb // LANES
    nt = seq // tb
    half = LANES // 2
    pos = jnp.arange(seq, dtype=F32)
    theta = 1.0 / ROPE_BASE ** jnp.linspace(0.0, 1.0, half, dtype=F32)
    ang = pos[:, None] * theta[None, :]
    cos, sin = jnp.cos(ang), jnp.sin(ang)
    cos2 = jnp.concatenate([cos, cos], axis=-1)
    sin2 = jnp.concatenate([-sin, sin], axis=-1)

    def colblk(cb):
        return pl.BlockSpec((tb, wb), lambda b, t: (b * nt + t, col0 + cb))

    return pl.pallas_call(
        functools.partial(_ret_kernel, n_heads=n_heads),
        grid=(bsz, nt),
        in_specs=[colblk(0), colblk(1), colblk(2), colblk(3),
                  pl.BlockSpec((tb, LANES), lambda b, t: (t, 0)),
                  pl.BlockSpec((tb, LANES), lambda b, t: (t, 0)),
                  pl.BlockSpec((1, wb), lambda b, t: (0, 0))],
        out_specs=pl.BlockSpec((tb, wb), lambda b, t: (b * nt + t, 0)),
        out_shape=jax.ShapeDtypeStruct((bsz * seq, wb), BF16),
        scratch_shapes=[pltpu.VMEM((n_heads, LANES, LANES), F32), pltpu.VMEM((n_heads, CHUNK, CHUNK), F32),
                        pltpu.VMEM((n_heads, CHUNK, LANES), F32), pltpu.VMEM((n_heads, CHUNK, LANES), F32)],
        compiler_params=_params(("parallel", "arbitrary")),
        name="retention",
    )(proj, proj, proj, proj, cos2, sin2, norm_b.astype(F32).reshape(1, wb))


def _rglru_kernel(y_ref, x_ref, cw_ref, cb_ref, wa_ref, ba_ref, wx_ref, bx_ref, lam_ref,
                  o_ref, tail_ref, h_ref, *, n_groups):
    C = CHUNK
    tb = x_ref.shape[0]

    @pl.when(pl.program_id(1) == 0)
    def _():
        tail_ref[...] = jnp.zeros_like(tail_ref)
        h_ref[...] = jnp.zeros_like(h_ref)

    rows = lax.broadcasted_iota(jnp.int32, (C, LANES), 0)

    def chunk(c, carry):
        r0 = pl.multiple_of(c * C, C)
        for g in range(n_groups):
            lo = g * LANES
            sl = slice(lo, lo + LANES)
            u = _conv_silu(tail_ref, x_ref, cw_ref, r0, lo, bias=cb_ref[:, sl], act=False)
            ub = u.astype(BF16)
            r_gate = jax.nn.sigmoid(_dot(ub, wa_ref[g]) + ba_ref[:, sl])
            i_gate = jax.nn.sigmoid(_dot(ub, wx_ref[g]) + bx_ref[:, sl])
            log_a = -RG_C * r_gate * _softplus(-lam_ref[:, sl])
            a = jnp.exp(log_a)
            th = jnp.tanh(log_a)
            b = jnp.sqrt(-2.0 * th / (1.0 - th)) * (i_gate * u)
            d = 1
            while d < C:
                keep = rows >= d
                b = b + a * jnp.where(keep, pltpu.roll(b, d, 0), 0.0)
                a = a * jnp.where(keep, pltpu.roll(a, d, 0), 1.0)
                d *= 2
            h = b + a * h_ref[:, sl]
            h_ref[:, sl] = h[C - 1:C]
            y = y_ref[pl.ds(r0, C), sl]
            o_ref[pl.ds(r0, C), sl] = (jax.nn.gelu(y, approximate=True) * h).astype(o_ref.dtype)
        tail_ref[...] = x_ref[pl.ds(r0 + C - SUBLANES, SUBLANES), :]
        return carry

    lax.fori_loop(0, tb // C, chunk, 0)


def _rglru(proj, conv_w, conv_b, rg_wa, rg_ba, rg_wx, rg_bx, rg_lambda, bsz, seq, wc, tb=256):
    n_groups = wc // LANES
    nt = seq // tb
    row = lambda p: p.astype(F32).reshape(1, wc)
    small = lambda shape: pl.BlockSpec(shape, lambda b, t: (0,) * len(shape))
    return pl.pallas_call(
        functools.partial(_rglru_kernel, n_groups=n_groups),
        grid=(bsz, nt),
        in_specs=[pl.BlockSpec((tb, wc), lambda b, t: (b * nt + t, 0)),
                  pl.BlockSpec((tb, wc), lambda b, t: (b * nt + t, 1)),
                  small((CONV_K, wc)), small((1, wc)),
                  small((n_groups, LANES, LANES)), small((1, wc)),
                  small((n_groups, LANES, LANES)), small((1, wc)), small((1, wc))],
        out_specs=pl.BlockSpec((tb, wc), lambda b, t: (b * nt + t, 0)),
        out_shape=jax.ShapeDtypeStruct((bsz * seq, wc), BF16),
        scratch_shapes=[pltpu.VMEM((SUBLANES, wc), F32), pltpu.VMEM((1, wc), F32)],
        compiler_params=_params(("parallel", "arbitrary")),
        name="rglru",
    )(proj, proj, conv_w.astype(F32), row(conv_b), rg_wa.astype(BF16), row(rg_ba),
      rg_wx.astype(BF16), row(rg_bx), row(rg_lambda))


def _hgrn2_kernel(q_ref, f_ref, i_ref, g_ref, lb_ref, nd_ref, o_ref, st_ref, gam_ref, *, n_heads):
    C = CHUNK
    tb = q_ref.shape[0]

    @pl.when(pl.program_id(1) == 0)
    def _():
        st_ref[...] = jnp.zeros_like(st_ref)

    row = lax.broadcasted_iota(jnp.int32, (C, C), 0)
    col = lax.broadcasted_iota(jnp.int32, (C, C), 1)
    rows = lax.broadcasted_iota(jnp.int32, (C, LANES), 0)
    gain = nd_ref[...]

    def chunk(c, carry):
        r0 = pl.multiple_of(c * C, C)
        for h in range(n_heads):
            lo = h * LANES
            sl = slice(lo, lo + LANES)
            lb = lb_ref[:, sl]
            f_lin = f_ref[pl.ds(r0, C), sl]
            log_f = jnp.logaddexp(jnp.log(lb), jnp.log1p(-lb) - _softplus(-f_lin))
            k = (1.0 - lb) * jax.nn.sigmoid(-f_lin)
            q = _silu(q_ref[pl.ds(r0, C), sl])
            v = i_ref[pl.ds(r0, C), sl]
            vb = v.astype(BF16)
            gam = _cumsum_rows(log_f)
            gam_ref[...] = gam
            att = jnp.where(row == col, jnp.sum(q * k, axis=-1, keepdims=True), 0.0)
            b = C // 2
            while b >= 1:
                nblk = C // (2 * b)
                if b >= SUBLANES // 2:
                    parts = [jnp.broadcast_to(gam_ref[pl.ds(j * 2 * b + b - 1, 1), :], (2 * b, LANES)) for j in range(nblk)]
                    gmid = jnp.concatenate(parts, axis=0) if nblk > 1 else parts[0]
                else:
                    off = (rows & (2 * b - 1)) - (b - 1)
                    gmid = jnp.zeros_like(gam)
                    for o_ in range(-(b - 1), b + 1):
                        gmid = jnp.where(off == o_, pltpu.roll(gam, o_ % C, 0), gmid)
                upper = (rows & (2 * b - 1)) >= b
                e = jnp.exp(-jnp.abs(gam - gmid))
                qt = jnp.where(upper, q * e, 0.0).astype(BF16)
                kt = jnp.where(upper, 0.0, k * e).astype(BF16)
                att = att + jnp.where((row & -(2 * b)) == (col & -(2 * b)), _dot_nt(qt, kt), 0.0)
                b //= 2
            st = st_ref[h]
            o = _dot(att.astype(BF16), vb) + _dot_nt((q * jnp.exp(gam)).astype(BF16), st.astype(BF16))
            glast = gam[C - 1:C, :]
            kg = (k * jnp.exp(glast - gam)).astype(BF16)
            st_ref[h] = st * jnp.exp(glast) + _dot_tn(vb, kg)
            g = g_ref[pl.ds(r0, C), sl]
            o_ref[pl.ds(r0, C), sl] = _rms_gate(o, gain, g).astype(o_ref.dtype)
        return carry

    lax.fori_loop(0, tb // C, chunk, 0)


def _hgrn2(proj, lb, norm_d, bsz, seq, wd, col0, tb=256):
    n_heads = wd // LANES
    nt = seq // tb

    def colblk(cb):
        return pl.BlockSpec((tb, wd), lambda b, t: (b * nt + t, col0 + cb))

    return pl.pallas_call(
        functools.partial(_hgrn2_kernel, n_heads=n_heads),
        grid=(bsz, nt),
        in_specs=[colblk(0), colblk(1), colblk(2), colblk(3),
                  pl.BlockSpec((1, wd), lambda b, t: (0, 0)), pl.BlockSpec((1, LANES), lambda b, t: (0, 0))],
        out_specs=pl.BlockSpec((tb, wd), lambda b, t: (b * nt + t, 0)),
        out_shape=jax.ShapeDtypeStruct((bsz * seq, wd), BF16),
        scratch_shapes=[pltpu.VMEM((n_heads, LANES, LANES), F32), pltpu.VMEM((CHUNK, LANES), F32)],
        compiler_params=_params(("parallel", "arbitrary")),
        name="hgrn2",
    )(proj, proj, proj, proj, lb.astype(F32).reshape(1, wd), norm_d.astype(F32).reshape(1, LANES))


def _router_kernel(x_ref, g_ref, wr_ref, hn_ref, r_ref):
    x = x_ref[...]
    ms = jnp.mean(x * x, axis=-1, keepdims=True)
    hn = x * lax.rsqrt(ms + EPS) * g_ref[...]
    hn_ref[...] = hn
    w = wr_ref[...]
    h1 = hn.astype(BF16)
    h2 = (hn - h1.astype(F32)).astype(BF16)
    h3 = (hn - h1.astype(F32) - h2.astype(F32)).astype(BF16)
    w1 = w.astype(BF16)
    w2 = (w - w1.astype(F32)).astype(BF16)
    w3 = (w - w1.astype(F32) - w2.astype(F32)).astype(BF16)
    logits = (_dot(h1, w1) + (_dot(h1, w2) + _dot(h2, w1))
              + (_dot(h2, w2) + _dot(h1, w3) + _dot(h3, w1)))
    lane = lax.broadcasted_iota(jnp.int32, logits.shape, 1)
    lg = jnp.where(lane < N_EXPERTS, logits, -jnp.inf)
    m1 = jnp.max(lg, axis=-1, keepdims=True)
    i1 = jnp.min(jnp.where(lg == m1, lane, LANES), axis=-1, keepdims=True)
    lg2 = jnp.where(lane == i1, -jnp.inf, lg)
    m2 = jnp.max(lg2, axis=-1, keepdims=True)
    i2 = jnp.min(jnp.where(lg2 == m2, lane, LANES), axis=-1, keepdims=True)
    e = jnp.exp(m2 - m1)
    g1 = 1.0 / (1.0 + e)
    g2 = e / (1.0 + e)
    r_ref[...] = jnp.where(lane == 0, i1.astype(F32), jnp.where(lane == 1, i2.astype(F32),
                           jnp.where(lane == 2, g1, jnp.where(lane == 3, g2, 0.0))))


def _router(x, g, w_router, tm=256):
    t, d = x.shape
    wr = jnp.zeros((d, LANES), F32).at[:, :N_EXPERTS].set(w_router.astype(F32))
    return pl.pallas_call(
        _router_kernel,
        grid=(t // tm,),
        in_specs=[pl.BlockSpec((tm, d), lambda i: (i, 0)), pl.BlockSpec((1, d), lambda i: (0, 0)),
                  pl.BlockSpec((d, LANES), lambda i: (0, 0))],
        out_specs=[pl.BlockSpec((tm, d), lambda i: (i, 0)), pl.BlockSpec((tm, LANES), lambda i: (i, 0))],
        out_shape=[jax.ShapeDtypeStruct((t, d), F32), jax.ShapeDtypeStruct((t, LANES), F32)],
        compiler_params=_params(("parallel",)),
        name="router",
    )(x, g.reshape(1, d), wr)


def _row_copy(src, s, dst, d, sem):
    return pltpu.make_async_copy(src.at[pl.ds(s, 1)], dst.at[pl.ds(d, 1)], sem)


def _dispatch_kernel(dest_ref, hn_ref, init_ref, xp_ref, sem, *, tb):
    del init_ref
    t0 = pl.program_id(0) * tb

    def start(r, carry):
        for k in range(TOP_K):
            _row_copy(hn_ref, t0 + r, xp_ref, dest_ref[0, 0, TOP_K * r + k], sem).start()
        return carry

    def wait(r, carry):
        for k in range(TOP_K):
            _row_copy(hn_ref, 0, xp_ref, 0, sem).wait()
        return carry

    lax.fori_loop(0, tb, start, 0)
    lax.fori_loop(0, tb, wait, 0)


def _dispatch(hn, dest, n_rows, tb=512):
    t, d = hn.shape
    return pl.pallas_call(
        functools.partial(_dispatch_kernel, tb=tb),
        grid=(t // tb,),
        in_specs=[pl.BlockSpec((1, 1, TOP_K * tb), lambda i: (i, 0, 0), memory_space=pltpu.SMEM),
                  pl.BlockSpec(memory_space=pl.ANY), pl.BlockSpec(memory_space=pl.ANY)],
        out_specs=pl.BlockSpec(memory_space=pl.ANY),
        out_shape=jax.ShapeDtypeStruct((n_rows, d), hn.dtype),
        scratch_shapes=[pltpu.SemaphoreType.DMA(())],
        input_output_aliases={2: 0},
        compiler_params=_params(("arbitrary",)),
        name="moe_dispatch",
    )(dest.reshape(t // tb, 1, TOP_K * tb), hn, jnp.zeros((n_rows, d), hn.dtype))


def _combine_kernel(dest_ref, r_ref, x_ref, y_ref, gf_ref, o_ref, buf_ref, sem, *, tb):
    def start(r, carry):
        for k in range(TOP_K):
            _row_copy(y_ref, dest_ref[0, 0, TOP_K * r + k], buf_ref.at[k], r, sem).start()
        return carry

    def wait(r, carry):
        for k in range(TOP_K):
            _row_copy(y_ref, 0, buf_ref.at[k], 0, sem).wait()
        return carry

    lax.fori_loop(0, tb, start, 0)
    lax.fori_loop(0, tb, wait, 0)
    gates = r_ref[...]
    acc = x_ref[...] + gates[:, 2:3] * buf_ref[0] + gates[:, 3:4] * buf_ref[1]
    ms = jnp.mean(acc * acc, axis=-1, keepdims=True)
    o_ref[...] = acc * lax.rsqrt(ms + EPS) * gf_ref[...]


def _combine(x, y_pad, dest, route, g_final, tb=256):
    t, d = x.shape
    return pl.pallas_call(
        functools.partial(_combine_kernel, tb=tb),
        grid=(t // tb,),
        in_specs=[pl.BlockSpec((1, 1, TOP_K * tb), lambda i: (i, 0, 0), memory_space=pltpu.SMEM),
                  pl.BlockSpec((tb, LANES), lambda i: (i, 0)),
                  pl.BlockSpec((tb, d), lambda i: (i, 0)),
                  pl.BlockSpec(memory_space=pl.ANY),
                  pl.BlockSpec((1, d), lambda i: (0, 0))],
        out_specs=pl.BlockSpec((tb, d), lambda i: (i, 0)),
        out_shape=jax.ShapeDtypeStruct((t, d), F32),
        scratch_shapes=[pltpu.VMEM((TOP_K, tb, d), F32), pltpu.SemaphoreType.DMA(())],
        compiler_params=_params(("arbitrary",)),
        name="moe_combine",
    )(dest.reshape(t // tb, 1, TOP_K * tb), route, x, y_pad, g_final.reshape(1, d))


def _routing_tables(route, tm):
    t = route.shape[0]
    flat_e = route[:, :TOP_K].astype(jnp.int32).reshape(-1)
    onehot = (flat_e[:, None] == jnp.arange(N_EXPERTS, dtype=jnp.int32)[None, :]).astype(jnp.int32)
    cum = jnp.cumsum(onehot, axis=0)
    rank = jnp.sum((cum - onehot) * onehot, axis=1)
    counts = cum[-1]
    padded = (counts + tm - 1) // tm * tm
    p_end = jnp.cumsum(padded)
    dest = (jnp.sum(onehot * (p_end - padded)[None, :], axis=1) + rank).astype(jnp.int32)
    n_blocks = -(-(t * TOP_K) // tm) + N_EXPERTS
    block_e = jnp.minimum(jnp.searchsorted(p_end, jnp.arange(n_blocks, dtype=jnp.int32) * tm, side='right'),
                          N_EXPERTS - 1).astype(jnp.int32)
    n_valid = (p_end[-1] // tm).astype(jnp.int32).reshape(1)
    return dest, block_e, n_valid, n_blocks


def kernel(x, norm_mix, norm_ffn, w_in_ab, conv_a, a_log, dt_bias, norm_a, norm_b, w_out_ab, w_gate_dense,
           w_up_dense, w_down_dense, w_in_cd, conv_c_w, conv_c_b, rg_wa, rg_ba, rg_wx, rg_bx, rg_lambda, hgrn_lb,
           norm_d, w_out_cd, w_router, w_gate_moe, w_up_moe, w_down_moe, norm_final):
    bsz, seq, d = x.shape
    xf = x.reshape(bsz * seq, d).astype(F32)
    x1 = _mixer_ab(xf, norm_mix[0], w_in_ab[0], conv_a[0], a_log[0], dt_bias[0], norm_a[0], norm_b[0], w_out_ab[0],
                   bsz, seq)
    x2 = _dense_ffn(x1, norm_ffn[0], w_gate_dense, w_up_dense, w_down_dense)
    lb_soft = jax.nn.softmax(hgrn_lb.astype(F32), axis=0)
    lb_all = jnp.cumsum(lb_soft, axis=0) - lb_soft[0:1]
    x3 = _mixer_cd(x2, norm_mix[1], w_in_cd[0], conv_c_w[0], conv_c_b[0], rg_wa[0], rg_ba[0], rg_wx[0], rg_bx[0],
                   rg_lambda[0], lb_all[1], norm_d[0], w_out_cd[0], bsz, seq)
    out = _moe_ffn_final(x3, norm_ffn[1], w_router[0], w_gate_moe[0], w_up_moe[0], w_down_moe[0], norm_final)
    return out.reshape(bsz, seq, d).astype(x.dtype)


def _mixer_ab(xf, g_mix, w_in, conv_a, a_log, dt_bias, norm_a, norm_b, w_out, bsz, seq):
    wa = xf.shape[1] // 2
    n_ha = wa // LANES
    pad = lambda cols: jnp.pad(cols, ((0, 0), (0, LANES - cols.shape[1])))
    w_perm = jnp.concatenate([w_in[:, :4 * wa], w_in[:, 4 * wa + 2 * n_ha:],
                              pad(w_in[:, 4 * wa:4 * wa + n_ha]), pad(w_in[:, 4 * wa + n_ha:4 * wa + 2 * n_ha])],
                             axis=1).astype(BF16)
    hn = _rms_cast(xf, g_mix)
    proj = _matmul([hn], [w_perm], tn=768, name="in_proj_ab")
    o_a = _gdn(proj, conv_a, a_log, dt_bias, norm_a, bsz, seq, wa)
    o_b = _retention(proj, norm_b, bsz, seq, wa, col0=4)
    wo = w_out.astype(BF16)
    return _matmul([o_a, o_b], [wo[:wa], wo[wa:]], res=xf, tm=512, tn=1024, name="out_proj_ab")


def _dense_ffn(x1, g_ffn, w_gate, w_up, w_down, tm=512):
    nb = x1.shape[0] // tm
    hn = _rms_cast(x1, g_ffn)
    return _ffn(hn, w_gate.astype(BF16), w_up.astype(BF16), w_down.astype(BF16),
                jnp.zeros((nb,), jnp.int32), jnp.full((1,), nb, jnp.int32), res=x1, tm=tm)


def _mixer_cd(x2, g_mix, w_in, conv_w, conv_b, rg_wa, rg_ba, rg_wx, rg_bx, rg_lambda, lb, norm_d, w_out, bsz, seq):
    wc = x2.shape[1] // 2
    hn = _rms_cast(x2, g_mix)
    proj = _matmul([hn], [w_in.astype(BF16)], tn=768, name="in_proj_cd")
    o_c = _rglru(proj, conv_w, conv_b, rg_wa, rg_ba, rg_wx, rg_bx, rg_lambda, bsz, seq, wc)
    o_d = _hgrn2(proj, lb, norm_d, bsz, seq, wc, col0=2)
    wo = w_out.astype(BF16)
    return _matmul([o_c, o_d], [wo[:wc], wo[wc:]], res=x2, tm=512, tn=1024, name="out_proj_cd")


def _moe_ffn_final(x3, g_ffn, w_router, w_gate, w_up, w_down, g_final, tm=512):
    hn_moe, route = _router(x3, g_ffn, w_router)
    dest, block_e, n_valid, n_blocks = _routing_tables(route, tm)
    x_pad = _dispatch(hn_moe, dest, n_blocks * tm)
    y_pad = _ffn(x_pad, w_gate.astype(BF16), w_up.astype(BF16), w_down.astype(BF16), block_e, n_valid, tm=tm)
    return _combine(x3, y_pad, dest, route, g_final)
```

```python
import functools
import math

import jax
import jax.numpy as jnp
from jax import lax
from jax.experimental import pallas as pl
from jax.experimental.pallas import tpu as pltpu

F32 = jnp.float32
BF16 = jnp.bfloat16
LANES = 128
SUBLANES = 8
EPS = 1e-6
CONV_K = 4
RG_C = 8.0
ROPE_BASE = 10000.0
N_EXPERTS = 8
TOP_K = 2
CHUNK = 128
VMEM_LIMIT = 48 * 2**20


def _params(sem):
    return pltpu.CompilerParams(dimension_semantics=sem, vmem_limit_bytes=VMEM_LIMIT)


def _dot(a, b):
    return jnp.dot(a, b, preferred_element_type=F32)


def _dot_nt(a, b):
    return lax.dot_general(a, b, (((1,), (1,)), ((), ())), preferred_element_type=F32)


def _dot_tn(a, b):
    return lax.dot_general(a, b, (((0,), (0,)), ((), ())), preferred_element_type=F32)


def _softplus(x):
    return jnp.maximum(x, 0.0) + jnp.log1p(jnp.exp(-jnp.abs(x)))


def _silu(x):
    return x * jax.nn.sigmoid(x)


def _cumsum_rows(x):
    n = x.shape[0]
    rows = lax.broadcasted_iota(jnp.int32, x.shape, 0)
    d = 1
    while d < n:
        x = x + jnp.where(rows >= d, pltpu.roll(x, d, 0), 0.0)
        d *= 2
    return x


def _rms_kernel(x_ref, g_ref, o_ref):
    x = x_ref[...]
    ms = jnp.mean(x * x, axis=-1, keepdims=True)
    o_ref[...] = (x * lax.rsqrt(ms + EPS) * g_ref[...]).astype(o_ref.dtype)


def _rms_cast(x, g, tm=512):
    t, d = x.shape
    out_dtype = BF16
    return pl.pallas_call(
        _rms_kernel,
        grid=(t // tm,),
        in_specs=[pl.BlockSpec((tm, d), lambda i: (i, 0)), pl.BlockSpec((1, d), lambda i: (0, 0))],
        out_specs=pl.BlockSpec((tm, d), lambda i: (i, 0)),
        out_shape=jax.ShapeDtypeStruct((t, d), out_dtype),
        compiler_params=_params(("parallel",)),
        name="rms_cast",
    )(x, g.reshape(1, d))


def _mm_kernel(*refs, n_a, has_res):
    o_ref = refs[-1]
    acc = _dot(refs[0][...], refs[n_a][...])
    for t in range(1, n_a):
        acc = acc + _dot(refs[t][...], refs[n_a + t][...])
    if has_res:
        acc = acc + refs[2 * n_a][...]
    o_ref[...] = acc.astype(o_ref.dtype)


def _matmul(a_list, w_list, res=None, out_dtype=F32, tm=1024, tn=512, name="matmul"):
    m = a_list[0].shape[0]
    n = w_list[0].shape[1]
    tm, tn = min(tm, m), min(tn, n)
    in_specs = [pl.BlockSpec((tm, a.shape[1]), lambda i, j: (i, 0)) for a in a_list]
    in_specs += [pl.BlockSpec((w.shape[0], tn), lambda i, j: (0, j)) for w in w_list]
    args = list(a_list) + list(w_list)
    if res is not None:
        in_specs.append(pl.BlockSpec((tm, tn), lambda i, j: (i, j)))
        args.append(res)
    return pl.pallas_call(
        functools.partial(_mm_kernel, n_a=len(a_list), has_res=res is not None),
        grid=(m // tm, n // tn),
        in_specs=in_specs,
        out_specs=pl.BlockSpec((tm, tn), lambda i, j: (i, j)),
        out_shape=jax.ShapeDtypeStruct((m, n), out_dtype),
        compiler_params=_params(("parallel", "arbitrary")),
        name=name,
    )(*args)


def _ffn_kernel(be_ref, nv_ref, x_ref, wg_ref, wu_ref, wd_ref, *rest, has_res):
    o_ref = rest[-1]
    i, j = pl.program_id(0), pl.program_id(1)
    valid = i < nv_ref[0]

    @pl.when(valid)
    def _():
        x = x_ref[...].astype(BF16)
        a = _silu(_dot(x, wg_ref[0])) * _dot(x, wu_ref[0])
        p = _dot(a.astype(BF16), wd_ref[0])

        @pl.when(j == 0)
        def _():
            o_ref[...] = p + rest[0][...] if has_res else p

        @pl.when(j > 0)
        def _():
            o_ref[...] += p

    @pl.when(jnp.logical_and(jnp.logical_not(valid), j == 0))
    def _():
        o_ref[...] = jnp.zeros_like(o_ref)


def _ffn(x, wg, wu, wd, block_e, n_valid, res=None, tm=512, tf=512):
    m, d = x.shape
    f = wg.shape[2]
    nb, nf = m // tm, f // tf

    def wcol(i, j, be, nv):
        return (be[i], 0, jnp.where(i < nv[0], j, nf - 1))

    def wrow(i, j, be, nv):
        return (be[i], jnp.where(i < nv[0], j, nf - 1), 0)

    in_specs = [
        pl.BlockSpec((tm, d), lambda i, j, be, nv: (i, 0)),
        pl.BlockSpec((1, d, tf), wcol),
        pl.BlockSpec((1, d, tf), wcol),
        pl.BlockSpec((1, tf, d), wrow),
    ]
    args = [x, wg, wu, wd]
    if res is not None:
        in_specs.append(pl.BlockSpec((tm, d), lambda i, j, be, nv: (i, 0)))
        args.append(res)
    return pl.pallas_call(
        functools.partial(_ffn_kernel, has_res=res is not None),
        grid_spec=pltpu.PrefetchScalarGridSpec(
            num_scalar_prefetch=2,
            grid=(nb, nf),
            in_specs=in_specs,
            out_specs=pl.BlockSpec((tm, d), lambda i, j, be, nv: (i, 0)),
        ),
        out_shape=jax.ShapeDtypeStruct((m, d), F32),
        compiler_params=_params(("parallel", "arbitrary")),
        name="ffn_swiglu",
    )(block_e, n_valid, *args)


def _conv_silu(tail_ref, x_ref, cw_ref, r0, lo, bias=None, act=True):
    sl = slice(lo, lo + LANES)
    win = jnp.concatenate([tail_ref[:, sl], x_ref[pl.ds(r0, CHUNK), sl]], axis=0)
    w = cw_ref[:, sl]
    y = win * w[CONV_K - 1:CONV_K]
    for k in range(1, CONV_K):
        y = y + pltpu.roll(win, k, 0) * w[CONV_K - 1 - k:CONV_K - k]
    y = y[SUBLANES:]
    if bias is not None:
        y = y + bias
    return _silu(y) if act else y


def _rms_gate(o, gain, z):
    ms = jnp.mean(o * o, axis=-1, keepdims=True)
    return o * lax.rsqrt(ms + EPS) * gain * _silu(z)


def _gdn_kernel(q_ref, k_ref, v_ref, z_ref, beta_ref, alpha_ref, cw_ref, alog_ref, dtb_ref, na_ref,
                o_ref, s_ref, tq_ref, tk_ref, tv_ref, *, n_heads):
    C = CHUNK
    tb = q_ref.shape[0]

    @pl.when(pl.program_id(1) == 0)
    def _():
        s_ref[...] = jnp.zeros_like(s_ref)
        tq_ref[...] = jnp.zeros_like(tq_ref)
        tk_ref[...] = jnp.zeros_like(tk_ref)
        tv_ref[...] = jnp.zeros_like(tv_ref)

    row = lax.broadcasted_iota(jnp.int32, (C, C), 0)
    col = lax.broadcasted_iota(jnp.int32, (C, C), 1)
    causal = row >= col
    strict = row > col
    eye = (row == col).astype(F32)
    neg_a = -jnp.exp(alog_ref[...])
    dtb = dtb_ref[...]
    gain = na_ref[...]
    scale = float(LANES) ** -0.5

    def chunk(c, carry):
        r0 = pl.multiple_of(c * C, C)
        beta_c = jax.nn.sigmoid(beta_ref[pl.ds(r0, C), :])
        g_c = neg_a * _softplus(alpha_ref[pl.ds(r0, C), :] + dtb)
        gam_c = _cumsum_rows(g_c)
        gam_t = gam_c.T
        heads = range(n_heads)
        low, att, rhs, wqg, kg, eglast = [], [], [], [], [], []
        for h in heads:
            lo = h * LANES
            q = _conv_silu(tq_ref, q_ref, cw_ref.at[0], r0, lo)
            k = _conv_silu(tk_ref, k_ref, cw_ref.at[1], r0, lo)
            v = _conv_silu(tv_ref, v_ref, cw_ref.at[2], r0, lo)
            q = q * lax.rsqrt(jnp.sum(q * q, axis=-1, keepdims=True) + EPS) * scale
            k = k * lax.rsqrt(jnp.sum(k * k, axis=-1, keepdims=True) + EPS)
            gcol = gam_c[:, h:h + 1]
            grow = gam_t[h:h + 1, :]
            bcol = beta_c[:, h:h + 1]
            dec = jnp.where(causal, jnp.exp(jnp.where(causal, gcol - grow, 0.0)), 0.0)
            kb = k * bcol
            m1 = _dot_nt(jnp.concatenate([kb, q], axis=0).astype(BF16), k.astype(BF16))
            low.append(jnp.where(strict, m1[:C] * dec, 0.0))
            att.append((m1[C:] * dec).astype(BF16))
            egam = jnp.exp(gcol)
            glast = gam_c[C - 1:C, h:h + 1]
            rhs.append(jnp.concatenate([v * bcol, kb * egam], axis=1).astype(BF16))
            wqg.append((q * egam).astype(BF16))
            kg.append((k * jnp.exp(glast - gcol)).astype(BF16))
            eglast.append(jnp.exp(glast))
        tinv = [eye - low[h] for h in heads]
        lp = [low[h].astype(BF16) for h in heads]
        for it in range(int(math.log2(C)) - 1):
            lp = [_dot(lp[h], lp[h]).astype(BF16) for h in heads]
            tinv = [tinv[h] + _dot(tinv[h].astype(BF16), lp[h]) for h in heads]
        sol = [_dot(tinv[h].astype(BF16), rhs[h]) for h in heads]
        s_old = [s_ref[h] for h in heads]
        wq = [_dot(jnp.concatenate([sol[h][:, LANES:].astype(BF16), wqg[h]], axis=0), s_old[h].astype(BF16))
              for h in heads]
        vnb = [(sol[h][:, :LANES] - wq[h][:C]).astype(BF16) for h in heads]
        for h in heads:
            s_ref[h] = eglast[h] * s_old[h] + _dot_tn(kg[h], vnb[h])
        for h in heads:
            lo = h * LANES
            o = wq[h][C:] + _dot(att[h], vnb[h])
            z = z_ref[pl.ds(r0, C), lo:lo + LANES]
            o_ref[pl.ds(r0, C), lo:lo + LANES] = _rms_gate(o, gain, z).astype(o_ref.dtype)
        tq_ref[...] = q_ref[pl.ds(r0 + C - SUBLANES, SUBLANES), :]
        tk_ref[...] = k_ref[pl.ds(r0 + C - SUBLANES, SUBLANES), :]
        tv_ref[...] = v_ref[pl.ds(r0 + C - SUBLANES, SUBLANES), :]
        return carry

    lax.fori_loop(0, tb // C, chunk, 0)


def _gdn(proj, conv_w, a_log, dt_bias, norm_a, bsz, seq, wa, tb=256):
    n_heads = wa // LANES
    nt = seq // tb
    gate_blk = 8 * wa // LANES

    def colblk(cb):
        return pl.BlockSpec((tb, wa), lambda b, t: (b * nt + t, cb))

    def pad_row(p):
        return jnp.zeros((1, LANES), F32).at[0, :n_heads].set(p.astype(F32))

    small = lambda shape: pl.BlockSpec(shape, lambda b, t: (0,) * len(shape))
    return pl.pallas_call(
        functools.partial(_gdn_kernel, n_heads=n_heads),
        grid=(bsz, nt),
        in_specs=[colblk(0), colblk(1), colblk(2), colblk(3),
                  pl.BlockSpec((tb, LANES), lambda b, t: (b * nt + t, gate_blk)),
                  pl.BlockSpec((tb, LANES), lambda b, t: (b * nt + t, gate_blk + 1)),
                  small((3, CONV_K, wa)), small((1, LANES)), small((1, LANES)), small((1, LANES))],
        out_specs=pl.BlockSpec((tb, wa), lambda b, t: (b * nt + t, 0)),
        out_shape=jax.ShapeDtypeStruct((bsz * seq, wa), BF16),
        scratch_shapes=[pltpu.VMEM((n_heads, LANES, LANES), F32),
                        pltpu.VMEM((SUBLANES, wa), F32), pltpu.VMEM((SUBLANES, wa), F32),
                        pltpu.VMEM((SUBLANES, wa), F32)],
        compiler_params=_params(("parallel", "arbitrary")),
        name="gdn",
    )(proj, proj, proj, proj, proj, proj,
      conv_w.astype(F32).reshape(CONV_K, 3, wa).transpose(1, 0, 2),
      pad_row(a_log), pad_row(dt_bias), norm_a.astype(F32).reshape(1, LANES))


def _ret_kernel(q_ref, k_ref, v_ref, g_ref, cos_ref, sin_ref, nb_ref, o_ref, r_ref, dm_ref, xi_ref, zeta_ref,
                *, n_heads):
    C = CHUNK
    tb = q_ref.shape[0]
    log_gammas = [math.log1p(-(2.0 ** (-5.0 - h))) for h in range(n_heads)]

    @pl.when(pl.program_id(1) == 0)
    def _():
        r_ref[...] = jnp.zeros_like(r_ref)
        row = lax.broadcasted_iota(jnp.int32, (C, C), 0)
        col = lax.broadcasted_iota(jnp.int32, (C, C), 1)
        rel = (row - col).astype(F32)
        idx = lax.broadcasted_iota(jnp.int32, (C, LANES), 0).astype(F32)
        for h in range(n_heads):
            dm_ref[h] = jnp.where(rel >= 0, jnp.exp(jnp.maximum(rel, 0.0) * log_gammas[h]), 0.0)
            xi_ref[h] = jnp.exp((idx + 1.0) * log_gammas[h])
            zeta_ref[h] = jnp.exp((C - 1.0 - idx) * log_gammas[h])

    scale = float(LANES) ** -0.5

    def chunk(c, carry):
        r0 = pl.multiple_of(c * C, C)
        cos = cos_ref[pl.ds(r0, C), :]
        sin = sin_ref[pl.ds(r0, C), :]
        for h in range(n_heads):
            lo = h * LANES
            sl = slice(lo, lo + LANES)
            g_chunk = math.exp(C * log_gammas[h])
            q = q_ref[pl.ds(r0, C), sl]
            k = k_ref[pl.ds(r0, C), sl]
            v = v_ref[pl.ds(r0, C), sl]
            q = q * cos + pltpu.roll(q, LANES // 2, 1) * sin
            k = (k * cos + pltpu.roll(k, LANES // 2, 1) * sin) * scale
            kb = k.astype(BF16)
            vb = v.astype(BF16)
            scores = _dot_nt(q.astype(BF16), kb) * dm_ref[h]
            r_h = r_ref[h]
            o = _dot(scores.astype(BF16), vb) + _dot((q * xi_ref[h]).astype(BF16), r_h.astype(BF16))
            r_ref[h] = g_chunk * r_h + _dot_tn((k * zeta_ref[h]).astype(BF16), vb)
            mu = jnp.mean(o, axis=-1, keepdims=True)
            oc = o - mu
            var = jnp.mean(oc * oc, axis=-1, keepdims=True)
            g = g_ref[pl.ds(r0, C), sl]
            o_ref[pl.ds(r0, C), sl] = (oc * lax.rsqrt(var + EPS) * nb_ref[:, sl] * _silu(g)).astype(o_ref.dtype)
        return carry

    lax.fori_loop(0, tb // C, chunk, 0)


def _retention(proj, norm_b, bsz, seq, wb, col0, tb=256):
    n_heads = wb // LANES
    nt = seq // tb
    half = LANES // 2
    pos = jnp.arange(seq, dtype=F32)
    theta = 1.0 / ROPE_BASE ** jnp.linspace(0.0, 1.0, half, dtype=F32)
    ang = pos[:, None] * theta[None, :]
    cos, sin = jnp.cos(ang), jnp.sin(ang)
    cos2 = jnp.concatenate([cos, cos], axis=-1)
    sin2 = jnp.concatenate([-sin, sin], axis=-1)

    def colblk(cb):
        return pl.BlockSpec((tb, wb), lambda b, t: (b * nt + t, col0 + cb))

    return pl.pallas_call(
        functools.partial(_ret_kernel, n_heads=n_heads),
        grid=(bsz, nt),
        in_specs=[colblk(0), colblk(1), colblk(2), colblk(3),
                  pl.BlockSpec((tb, LANES), lambda b, t: (t, 0)),
                  pl.BlockSpec((tb, LANES), lambda b, t: (t, 0)),
                  pl.BlockSpec((1, wb), lambda b, t: (0, 0))],
        out_specs=pl.BlockSpec((tb, wb), lambda b, t: (b * nt + t, 0)),
        out_shape=jax.ShapeDtypeStruct((bsz * seq, wb), BF16),
        scratch_shapes=[pltpu.VMEM((n_heads, LANES, LANES), F32), pltpu.VMEM((n_heads, CHUNK, CHUNK), F32),
                        pltpu.VMEM((n_heads, CHUNK, LANES), F32), pltpu.VMEM((n_heads, CHUNK, LANES), F32)],
        compiler_params=_params(("parallel", "arbitrary")),
        name="retention",
    )(proj, proj, proj, proj, cos2, sin2, norm_b.astype(F32).reshape(1, wb))


def _rglru_kernel(y_ref, x_ref, cw_ref, cb_ref, wa_ref, ba_ref, wx_ref, bx_ref, lam_ref,
                  o_ref, tail_ref, h_ref, *, n_groups):
    C = CHUNK
    tb = x_ref.shape[0]

    @pl.when(pl.program_id(1) == 0)
    def _():
        tail_ref[...] = jnp.zeros_like(tail_ref)
        h_ref[...] = jnp.zeros_like(h_ref)

    rows = lax.broadcasted_iota(jnp.int32, (C, LANES), 0)

    def chunk(c, carry):
        r0 = pl.multiple_of(c * C, C)
        for g in range(n_groups):
            lo = g * LANES
            sl = slice(lo, lo + LANES)
            u = _conv_silu(tail_ref, x_ref, cw_ref, r0, lo, bias=cb_ref[:, sl], act=False)
            ub = u.astype(BF16)
            r_gate = jax.nn.sigmoid(_dot(ub, wa_ref[g]) + ba_ref[:, sl])
            i_gate = jax.nn.sigmoid(_dot(ub, wx_ref[g]) + bx_ref[:, sl])
            log_a = -RG_C * r_gate * _softplus(-lam_ref[:, sl])
            a = jnp.exp(log_a)
            th = jnp.tanh(log_a)
            b = jnp.sqrt(-2.0 * th / (1.0 - th)) * (i_gate * u)
            d = 1
            while d < C:
                keep = rows >= d
                b = b + a * jnp.where(keep, pltpu.roll(b, d, 0), 0.0)
                a = a * jnp.where(keep, pltpu.roll(a, d, 0), 1.0)
                d *= 2
            h = b + a * h_ref[:, sl]
            h_ref[:, sl] = h[C - 1:C]
            y = y_ref[pl.ds(r0, C), sl]
            o_ref[pl.ds(r0, C), sl] = (jax.nn.gelu(y, approximate=True) * h).astype(o_ref.dtype)
        tail_ref[...] = x_ref[pl.ds(r0 + C - SUBLANES, SUBLANES), :]
        return carry

    lax.fori_loop(0, tb // C, chunk, 0)


def _rglru(proj, conv_w, conv_b, rg_wa, rg_ba, rg_wx, rg_bx, rg_lambda, bsz, seq, wc, tb=256):
    n_groups = wc // LANES
    nt = seq // tb
    row = lambda p: p.astype(F32).reshape(1, wc)
    small = lambda shape: pl.BlockSpec(shape, lambda b, t: (0,) * len(shape))
    return pl.pallas_call(
        functools.partial(_rglru_kernel, n_groups=n_groups),
        grid=(bsz, nt),
        in_specs=[pl.BlockSpec((tb, wc), lambda b, t: (b * nt + t, 0)),
                  pl.BlockSpec((tb, wc), lambda b, t: (b * nt + t, 1)),
                  small((CONV_K, wc)), small((1, wc)),
                  small((n_groups, LANES, LANES)), small((1, wc)),
                  small((n_groups, LANES, LANES)), small((1, wc)), small((1, wc))],
        out_specs=pl.BlockSpec((tb, wc), lambda b, t: (b * nt + t, 0)),
        out_shape=jax.ShapeDtypeStruct((bsz * seq, wc), BF16),
        scratch_shapes=[pltpu.VMEM((SUBLANES, wc), F32), pltpu.VMEM((1, wc), F32)],
        compiler_params=_params(("parallel", "arbitrary")),
        name="rglru",
    )(proj, proj, conv_w.astype(F32), row(conv_b), rg_wa.astype(BF16), row(rg_ba),
      rg_wx.astype(BF16), row(rg_bx), row(rg_lambda))


def _hgrn2_kernel(q_ref, f_ref, i_ref, g_ref, lb_ref, nd_ref, o_ref, st_ref, gam_ref, *, n_heads):
    C = CHUNK
    tb = q_ref.shape[0]

    @pl.when(pl.program_id(1) == 0)
    def _():
        st_ref[...] = jnp.zeros_like(st_ref)

    row = lax.broadcasted_iota(jnp.int32, (C, C), 0)
    col = lax.broadcasted_iota(jnp.int32, (C, C), 1)
    rows = lax.broadcasted_iota(jnp.int32, (C, LANES), 0)
    gain = nd_ref[...]

    def chunk(c, carry):
        r0 = pl.multiple_of(c * C, C)
        for h in range(n_heads):
            lo = h * LANES
            sl = slice(lo, lo + LANES)
            lb = lb_ref[:, sl]
            f_lin = f_ref[pl.ds(r0, C), sl]
            log_f = jnp.logaddexp(jnp.log(lb), jnp.log1p(-lb) - _softplus(-f_lin))
            k = (1.0 - lb) * jax.nn.sigmoid(-f_lin)
            q = _silu(q_ref[pl.ds(r0, C), sl])
            v = i_ref[pl.ds(r0, C), sl]
            vb = v.astype(BF16)
            gam = _cumsum_rows(log_f)
            gam_ref[...] = gam
            att = jnp.where(row == col, jnp.sum(q * k, axis=-1, keepdims=True), 0.0)
            b = C // 2
            while b >= 1:
                nblk = C // (2 * b)
                if b >= SUBLANES // 2:
                    parts = [jnp.broadcast_to(gam_ref[pl.ds(j * 2 * b + b - 1, 1), :], (2 * b, LANES)) for j in range(nblk)]
                    gmid = jnp.concatenate(parts, axis=0) if nblk > 1 else parts[0]
                else:
                    off = (rows & (2 * b - 1)) - (b - 1)
                    gmid = jnp.zeros_like(gam)
                    for o_ in range(-(b - 1), b + 1):
                        gmid = jnp.where(off == o_, pltpu.roll(gam, o_ % C, 0), gmid)
                upper = (rows & (2 * b - 1)) >= b
                e = jnp.exp(-jnp.abs(gam - gmid))
                qt = jnp.where(upper, q * e, 0.0).astype(BF16)
                kt = jnp.where(upper, 0.0, k * e).astype(BF16)
                att = att + jnp.where((row & -(2 * b)) == (col & -(2 * b)), _dot_nt(qt, kt), 0.0)
                b //= 2
            st = st_ref[h]
            o = _dot(att.astype(BF16), vb) + _dot_nt((q * jnp.exp(gam)).astype(BF16), st.astype(BF16))
            glast = gam[C - 1:C, :]
            kg = (k * jnp.exp(glast - gam)).astype(BF16)
            st_ref[h] = st * jnp.exp(glast) + _dot_tn(vb, kg)
            g = g_ref[pl.ds(r0, C), sl]
            o_ref[pl.ds(r0, C), sl] = _rms_gate(o, gain, g).astype(o_ref.dtype)
        return carry

    lax.fori_loop(0, tb // C, chunk, 0)


def _hgrn2(proj, lb, norm_d, bsz, seq, wd, col0, tb=256):
    n_heads = wd // LANES
    nt = seq // tb

    def colblk(cb):
        return pl.BlockSpec((tb, wd), lambda b, t: (b * nt + t, col0 + cb))

    return pl.pallas_call(
        functools.partial(_hgrn2_kernel, n_heads=n_heads),
        grid=(bsz, nt),
        in_specs=[colblk(0), colblk(1), colblk(2), colblk(3),
                  pl.BlockSpec((1, wd), lambda b, t: (0, 0)), pl.BlockSpec((1, LANES), lambda b, t: (0, 0))],
        out_specs=pl.BlockSpec((tb, wd), lambda b, t: (b * nt + t, 0)),
        out_shape=jax.ShapeDtypeStruct((bsz * seq, wd), BF16),
        scratch_shapes=[pltpu.VMEM((n_heads, LANES, LANES), F32), pltpu.VMEM((CHUNK, LANES), F32)],
        compiler_params=_params(("parallel", "arbitrary")),
        name="hgrn2",
    )(proj, proj, proj, proj, lb.astype(F32).reshape(1, wd), norm_d.astype(F32).reshape(1, LANES))


def _router_kernel(x_ref, g_ref, wr_ref, hn_ref, r_ref):
    x = x_ref[...]
    ms = jnp.mean(x * x, axis=-1, keepdims=True)
    hn = x * lax.rsqrt(ms + EPS) * g_ref[...]
    hn_ref[...] = hn
    w = wr_ref[...]
    h1 = hn.astype(BF16)
    h2 = (hn - h1.astype(F32)).astype(BF16)
    h3 = (hn - h1.astype(F32) - h2.astype(F32)).astype(BF16)
    w1 = w.astype(BF16)
    w2 = (w - w1.astype(F32)).astype(BF16)
    w3 = (w - w1.astype(F32) - w2.astype(F32)).astype(BF16)
    logits = (_dot(h1, w1) + (_dot(h1, w2) + _dot(h2, w1))
              + (_dot(h2, w2) + _dot(h1, w3) + _dot(h3, w1)))
    lane = lax.broadcasted_iota(jnp.int32, logits.shape, 1)
    lg = jnp.where(lane < N_EXPERTS, logits, -jnp.inf)
    m1 = jnp.max(lg, axis=-1, keepdims=True)
    i1 = jnp.min(jnp.where(lg == m1, lane, LANES), axis=-1, keepdims=True)
    lg2 = jnp.where(lane == i1, -jnp.inf, lg)
    m2 = jnp.max(lg2, axis=-1, keepdims=True)
    i2 = jnp.min(jnp.where(lg2 == m2, lane, LANES), axis=-1, keepdims=True)
    e = jnp.exp(m2 - m1)
    g1 = 1.0 / (1.0 + e)
    g2 = e / (1.0 + e)
    r_ref[...] = jnp.where(lane == 0, i1.astype(F32), jnp.where(lane == 1, i2.astype(F32),
                           jnp.where(lane == 2, g1, jnp.where(lane == 3, g2, 0.0))))


def _router(x, g, w_router, tm=256):
    t, d = x.shape
    wr = jnp.zeros((d, LANES), F32).at[:, :N_EXPERTS].set(w_router.astype(F32))
    return pl.pallas_call(
        _router_kernel,
        grid=(t // tm,),
        in_specs=[pl.BlockSpec((tm, d), lambda i: (i, 0)), pl.BlockSpec((1, d), lambda i: (0, 0)),
                  pl.BlockSpec((d, LANES), lambda i: (0, 0))],
        out_specs=[pl.BlockSpec((tm, d), lambda i: (i, 0)), pl.BlockSpec((tm, LANES), lambda i: (i, 0))],
        out_shape=[jax.ShapeDtypeStruct((t, d), F32), jax.ShapeDtypeStruct((t, LANES), F32)],
        compiler_params=_params(("parallel",)),
        name="router",
    )(x, g.reshape(1, d), wr)


def _row_copy(src, s, dst, d, sem):
    return pltpu.make_async_copy(src.at[pl.ds(s, 1)], dst.at[pl.ds(d, 1)], sem)


def _dispatch_kernel(nv_ref, src_ref, hn_ref, o_ref, buf_ref, sem, *, tb):
    valid = pl.program_id(0) < nv_ref[0]

    @pl.when(valid)
    def _():
        def start(r, carry):
            _row_copy(hn_ref, src_ref[0, 0, r], buf_ref, r, sem).start()
            return carry

        def wait(r, carry):
            _row_copy(hn_ref, 0, buf_ref, 0, sem).wait()
            return carry

        lax.fori_loop(0, tb, start, 0)
        lax.fori_loop(0, tb, wait, 0)
        o_ref[...] = buf_ref[...].astype(o_ref.dtype)

    @pl.when(jnp.logical_not(valid))
    def _():
        o_ref[...] = jnp.zeros_like(o_ref)


def _dispatch(hn, src, n_valid, tb=512):
    d = hn.shape[1]
    n_rows = src.shape[0]
    return pl.pallas_call(
        functools.partial(_dispatch_kernel, tb=tb),
        grid_spec=pltpu.PrefetchScalarGridSpec(
            num_scalar_prefetch=1,
            grid=(n_rows // tb,),
            in_specs=[pl.BlockSpec((1, 1, tb), lambda i, nv: (i, 0, 0), memory_space=pltpu.SMEM),
                      pl.BlockSpec(memory_space=pl.ANY)],
            out_specs=pl.BlockSpec((tb, d), lambda i, nv: (i, 0)),
            scratch_shapes=[pltpu.VMEM((tb, d), hn.dtype), pltpu.SemaphoreType.DMA(())],
        ),
        out_shape=jax.ShapeDtypeStruct((n_rows, d), BF16),
        compiler_params=_params(("arbitrary",)),
        name="moe_dispatch",
    )(n_valid, src.reshape(n_rows // tb, 1, tb), hn)


def _combine_kernel(dest_ref, r_ref, x_ref, y_ref, gf_ref, o_ref, buf_ref, sem, *, tb):
    def start(r, carry):
        for k in range(TOP_K):
            _row_copy(y_ref, dest_ref[0, 0, TOP_K * r + k], buf_ref.at[k], r, sem).start()
        return carry

    def wait(r, carry):
        for k in range(TOP_K):
            _row_copy(y_ref, 0, buf_ref.at[k], 0, sem).wait()
        return carry

    lax.fori_loop(0, tb, start, 0)
    lax.fori_loop(0, tb, wait, 0)
    gates = r_ref[...]
    acc = x_ref[...] + gates[:, 2:3] * buf_ref[0] + gates[:, 3:4] * buf_ref[1]
    ms = jnp.mean(acc * acc, axis=-1, keepdims=True)
    o_ref[...] = acc * lax.rsqrt(ms + EPS) * gf_ref[...]


def _combine(x, y_pad, dest, route, g_final, tb=256):
    t, d = x.shape
    return pl.pallas_call(
        functools.partial(_combine_kernel, tb=tb),
        grid=(t // tb,),
        in_specs=[pl.BlockSpec((1, 1, TOP_K * tb), lambda i: (i, 0, 0), memory_space=pltpu.SMEM),
                  pl.BlockSpec((tb, LANES), lambda i: (i, 0)),
                  pl.BlockSpec((tb, d), lambda i: (i, 0)),
                  pl.BlockSpec(memory_space=pl.ANY),
                  pl.BlockSpec((1, d), lambda i: (0, 0))],
        out_specs=pl.BlockSpec((tb, d), lambda i: (i, 0)),
        out_shape=jax.ShapeDtypeStruct((t, d), F32),
        scratch_shapes=[pltpu.VMEM((TOP_K, tb, d), F32), pltpu.SemaphoreType.DMA(())],
        compiler_params=_params(("arbitrary",)),
        name="moe_combine",
    )(dest.reshape(t // tb, 1, TOP_K * tb), route, x, y_pad, g_final.reshape(1, d))


def _routing_tables(route, tm):
    t = route.shape[0]
    flat_e = route[:, :TOP_K].astype(jnp.int32).reshape(-1)
    onehot = (flat_e[:, None] == jnp.arange(N_EXPERTS, dtype=jnp.int32)[None, :]).astype(jnp.int32)
    cum = jnp.cumsum(onehot, axis=0)
    rank = jnp.sum((cum - onehot) * onehot, axis=1)
    counts = cum[-1]
    padded = (counts + tm - 1) // tm * tm
    p_end = jnp.cumsum(padded)
    p_start = p_end - padded
    dest = (jnp.sum(onehot * p_start[None, :], axis=1) + rank).astype(jnp.int32)
    n_blocks = -(-(t * TOP_K) // tm) + N_EXPERTS
    block_e = jnp.minimum(jnp.searchsorted(p_end, jnp.arange(n_blocks, dtype=jnp.int32) * tm, side='right'),
                          N_EXPERTS - 1).astype(jnp.int32)
    n_valid = (p_end[-1] // tm).astype(jnp.int32).reshape(1)
    order = jnp.argsort(flat_e, stable=True).astype(jnp.int32)
    rows = jnp.arange(n_blocks * tm, dtype=jnp.int32)
    row_e = jnp.repeat(block_e, tm)
    row_rank = rows - p_start[row_e]
    entry = order[jnp.clip((jnp.cumsum(counts) - counts)[row_e] + row_rank, 0, t * TOP_K - 1)]
    src = jnp.where(row_rank < counts[row_e], entry // TOP_K, 0).astype(jnp.int32)
    return dest, src, block_e, n_valid


def kernel(x, norm_mix, norm_ffn, w_in_ab, conv_a, a_log, dt_bias, norm_a, norm_b, w_out_ab, w_gate_dense,
           w_up_dense, w_down_dense, w_in_cd, conv_c_w, conv_c_b, rg_wa, rg_ba, rg_wx, rg_bx, rg_lambda, hgrn_lb,
           norm_d, w_out_cd, w_router, w_gate_moe, w_up_moe, w_down_moe, norm_final):
    bsz, seq, d = x.shape
    xf = x.reshape(bsz * seq, d).astype(F32)
    x1 = _mixer_ab(xf, norm_mix[0], w_in_ab[0], conv_a[0], a_log[0], dt_bias[0], norm_a[0], norm_b[0], w_out_ab[0],
                   bsz, seq)
    x2 = _dense_ffn(x1, norm_ffn[0], w_gate_dense, w_up_dense, w_down_dense)
    lb_soft = jax.nn.softmax(hgrn_lb.astype(F32), axis=0)
    lb_all = jnp.cumsum(lb_soft, axis=0) - lb_soft[0:1]
    x3 = _mixer_cd(x2, norm_mix[1], w_in_cd[0], conv_c_w[0], conv_c_b[0], rg_wa[0], rg_ba[0], rg_wx[0], rg_bx[0],
                   rg_lambda[0], lb_all[1], norm_d[0], w_out_cd[0], bsz, seq)
    out = _moe_ffn_final(x3, norm_ffn[1], w_router[0], w_gate_moe[0], w_up_moe[0], w_down_moe[0], norm_final)
    return out.reshape(bsz, seq, d).astype(x.dtype)


def _mixer_ab(xf, g_mix, w_in, conv_a, a_log, dt_bias, norm_a, norm_b, w_out, bsz, seq):
    wa = xf.shape[1] // 2
    n_ha = wa // LANES
    pad = lambda cols: jnp.pad(cols, ((0, 0), (0, LANES - cols.shape[1])))
    w_perm = jnp.concatenate([w_in[:, :4 * wa], w_in[:, 4 * wa + 2 * n_ha:],
                              pad(w_in[:, 4 * wa:4 * wa + n_ha]), pad(w_in[:, 4 * wa + n_ha:4 * wa + 2 * n_ha])],
                             axis=1).astype(BF16)
    hn = _rms_cast(xf, g_mix)
    proj = _matmul([hn], [w_perm], tn=768, name="in_proj_ab")
    o_a = _gdn(proj, conv_a, a_log, dt_bias, norm_a, bsz, seq, wa)
    o_b = _retention(proj, norm_b, bsz, seq, wa, col0=4)
    wo = w_out.astype(BF16)
    return _matmul([o_a, o_b], [wo[:wa], wo[wa:]], res=xf, tm=512, tn=1024, name="out_proj_ab")


def _dense_ffn(x1, g_ffn, w_gate, w_up, w_down, tm=512):
    nb = x1.shape[0] // tm
    hn = _rms_cast(x1, g_ffn)
    return _ffn(hn, w_gate.astype(BF16), w_up.astype(BF16), w_down.astype(BF16),
                jnp.zeros((nb,), jnp.int32), jnp.full((1,), nb, jnp.int32), res=x1, tm=tm)


def _mixer_cd(x2, g_mix, w_in, conv_w, conv_b, rg_wa, rg_ba, rg_wx, rg_bx, rg_lambda, lb, norm_d, w_out, bsz, seq):
    wc = x2.shape[1] // 2
    hn = _rms_cast(x2, g_mix)
    proj = _matmul([hn], [w_in.astype(BF16)], tn=768, name="in_proj_cd")
    o_c = _rglru(proj, conv_w, conv_b, rg_wa, rg_ba, rg_wx, rg_bx, rg_lambda, bsz, seq, wc)
    o_d = _hgrn2(proj, lb, norm_d, bsz, seq, wc, col0=2)
    wo = w_out.astype(BF16)
    return _matmul([o_c, o_d], [wo[:wc], wo[wc:]], res=x2, tm=512, tn=1024, name="out_proj_cd")


def _moe_ffn_final(x3, g_ffn, w_router, w_gate, w_up, w_down, g_final, tm=512):
    hn_moe, route = _router(x3, g_ffn, w_router)
    dest, src, block_e, n_valid = _routing_tables(route, tm)
    x_pad = _dispatch(hn_moe, src, n_valid, tb=tm)
    y_pad = _ffn(x_pad, w_gate.astype(BF16), w_up.astype(BF16), w_down.astype(BF16), block_e, n_valid, tm=tm)
    return _combine(x3, y_pad, dest, route, g_final)
```

```python
import functools
import math

import jax
import jax.numpy as jnp
from jax import lax
from jax.experimental import pallas as pl
from jax.experimental.pallas import tpu as pltpu

F32 = jnp.float32
BF16 = jnp.bfloat16
LANES = 128
SUBLANES = 8
EPS = 1e-6
CONV_K = 4
RG_C = 8.0
ROPE_BASE = 10000.0
N_EXPERTS = 8
TOP_K = 2
CHUNK = 128
VMEM_LIMIT = 48 * 2**20


def _params(sem):
    return pltpu.CompilerParams(dimension_semantics=sem, vmem_limit_bytes=VMEM_LIMIT)


def _dot(a, b):
    return jnp.dot(a, b, preferred_element_type=F32)


def _dot_nt(a, b):
    return lax.dot_general(a, b, (((1,), (1,)), ((), ())), preferred_element_type=F32)


def _dot_tn(a, b):
    return lax.dot_general(a, b, (((0,), (0,)), ((), ())), preferred_element_type=F32)


def _softplus(x):
    return jnp.maximum(x, 0.0) + jnp.log1p(jnp.exp(-jnp.abs(x)))


def _silu(x):
    return x * jax.nn.sigmoid(x)


def _cumsum_rows(x):
    n = x.shape[0]
    rows = lax.broadcasted_iota(jnp.int32, x.shape, 0)
    d = 1
    while d < n:
        x = x + jnp.where(rows >= d, pltpu.roll(x, d, 0), 0.0)
        d *= 2
    return x


def _rms(x, g):
    ms = jnp.mean(x * x, axis=-1, keepdims=True)
    return x * lax.rsqrt(ms + EPS) * g


def _rms_mm_kernel(x_ref, g_ref, w_ref, o_ref, hn_ref):
    @pl.when(pl.program_id(1) == 0)
    def _():
        hn_ref[...] = _rms(x_ref[...], g_ref[...]).astype(hn_ref.dtype)

    o_ref[...] = _dot(hn_ref[...], w_ref[...])


def _rms_matmul(x, g, w, tm=1024, tn=768, name="rms_matmul"):
    m, k = x.shape
    n = w.shape[1]
    tm = min(tm, m)
    return pl.pallas_call(
        _rms_mm_kernel,
        grid=(m // tm, n // tn),
        in_specs=[pl.BlockSpec((tm, k), lambda i, j: (i, 0)), pl.BlockSpec((1, k), lambda i, j: (0, 0)),
                  pl.BlockSpec((k, tn), lambda i, j: (0, j))],
        out_specs=pl.BlockSpec((tm, tn), lambda i, j: (i, j)),
        out_shape=jax.ShapeDtypeStruct((m, n), F32),
        scratch_shapes=[pltpu.VMEM((tm, k), BF16)],
        compiler_params=_params(("parallel", "arbitrary")),
        name=name,
    )(x, g.astype(F32).reshape(1, k), w)


def _mm_kernel(*refs, n_a, has_res):
    o_ref = refs[-1]
    acc = _dot(refs[0][...], refs[n_a][...])
    for t in range(1, n_a):
        acc = acc + _dot(refs[t][...], refs[n_a + t][...])
    if has_res:
        acc = acc + refs[2 * n_a][...]
    o_ref[...] = acc.astype(o_ref.dtype)


def _matmul(a_list, w_list, res=None, out_dtype=F32, tm=1024, tn=512, name="matmul"):
    m = a_list[0].shape[0]
    n = w_list[0].shape[1]
    tm, tn = min(tm, m), min(tn, n)
    in_specs = [pl.BlockSpec((tm, a.shape[1]), lambda i, j: (i, 0)) for a in a_list]
    in_specs += [pl.BlockSpec((w.shape[0], tn), lambda i, j: (0, j)) for w in w_list]
    args = list(a_list) + list(w_list)
    if res is not None:
        in_specs.append(pl.BlockSpec((tm, tn), lambda i, j: (i, j)))
        args.append(res)
    return pl.pallas_call(
        functools.partial(_mm_kernel, n_a=len(a_list), has_res=res is not None),
        grid=(m // tm, n // tn),
        in_specs=in_specs,
        out_specs=pl.BlockSpec((tm, tn), lambda i, j: (i, j)),
        out_shape=jax.ShapeDtypeStruct((m, n), out_dtype),
        compiler_params=_params(("parallel", "arbitrary")),
        name=name,
    )(*args)


def _swiglu_accumulate(x, wg_ref, wu_ref, wd_ref, o_ref):
    a = _silu(_dot(x, wg_ref[0])) * _dot(x, wu_ref[0])
    o_ref[...] += _dot(a.astype(BF16), wd_ref[0])


def _dense_ffn_kernel(x_ref, g_ref, wg_ref, wu_ref, wd_ref, o_ref, hn_ref):
    @pl.when(pl.program_id(1) == 0)
    def _():
        x = x_ref[...]
        o_ref[...] = x
        hn_ref[...] = _rms(x, g_ref[...]).astype(hn_ref.dtype)

    _swiglu_accumulate(hn_ref[...], wg_ref, wu_ref, wd_ref, o_ref)


def _dense_ffn(x, g, wg, wu, wd, tm=512, tf=512):
    m, d = x.shape
    f = wg.shape[2]
    return pl.pallas_call(
        _dense_ffn_kernel,
        grid=(m // tm, f // tf),
        in_specs=[pl.BlockSpec((tm, d), lambda i, j: (i, 0)),
                  pl.BlockSpec((1, d), lambda i, j: (0, 0)),
                  pl.BlockSpec((1, d, tf), lambda i, j: (0, 0, j)),
                  pl.BlockSpec((1, d, tf), lambda i, j: (0, 0, j)),
                  pl.BlockSpec((1, tf, d), lambda i, j: (0, j, 0))],
        out_specs=pl.BlockSpec((tm, d), lambda i, j: (i, 0)),
        out_shape=jax.ShapeDtypeStruct((m, d), F32),
        scratch_shapes=[pltpu.VMEM((tm, d), BF16)],
        compiler_params=_params(("parallel", "arbitrary")),
        name="ffn_dense",
    )(x, g.astype(F32).reshape(1, d), wg, wu, wd)


def _row_copy(src, s, dst, d, sem):
    return pltpu.make_async_copy(src.at[pl.ds(s, 1)], dst.at[pl.ds(d, 1)], sem)


DMA_UNROLL = 8


def _gather_rows(table_ref, n_rows, src, dst, sem):
    def body(r, carry):
        _row_copy(src, table_ref[0, 0, r], dst, r, sem).start()
        return carry
    lax.fori_loop(0, n_rows, body, 0, unroll=DMA_UNROLL)


def _wait_rows(n_rows, src, dst, sem):
    def body(r, carry):
        _row_copy(src, 0, dst, 0, sem).wait()
        return carry
    lax.fori_loop(0, n_rows, body, 0, unroll=DMA_UNROLL)


def _moe_ffn_kernel(be_ref, nv_ref, src_ref, src_next_ref, hn_ref, wg_ref, wu_ref, wd_ref, o_ref,
                    rows_ref, x_ref, sems):
    i, j = pl.program_id(0), pl.program_id(1)
    n_valid = nv_ref[0]
    valid = i < n_valid
    tm = x_ref.shape[0]
    slot = lax.rem(i, 2)

    @pl.when(j == 0)
    def _():
        o_ref[...] = jnp.zeros_like(o_ref)

    @pl.when(jnp.logical_and(valid, j == 0))
    def _():
        @pl.when(i == 0)
        def _():
            _gather_rows(src_ref, tm, hn_ref, rows_ref.at[0], sems.at[0])

        _wait_rows(tm, hn_ref, rows_ref.at[slot], sems.at[slot])
        x_ref[...] = rows_ref[slot].astype(BF16)

        @pl.when(i + 1 < n_valid)
        def _():
            _gather_rows(src_next_ref, tm, hn_ref, rows_ref.at[1 - slot], sems.at[1 - slot])

    @pl.when(valid)
    def _():
        _swiglu_accumulate(x_ref[...], wg_ref, wu_ref, wd_ref, o_ref)


def _moe_ffn_call(hn, src, wg, wu, wd, block_e, n_valid, tm=512, tf=512):
    d = hn.shape[1]
    f = wg.shape[2]
    nb, nf = src.shape[0] // tm, f // tf
    src3 = src.reshape(nb, 1, tm)

    def wcol(i, j, be, nv):
        return (be[i], 0, jnp.where(i < nv[0], j, nf - 1))

    def wrow(i, j, be, nv):
        return (be[i], jnp.where(i < nv[0], j, nf - 1), 0)

    return pl.pallas_call(
        _moe_ffn_kernel,
        grid_spec=pltpu.PrefetchScalarGridSpec(
            num_scalar_prefetch=2,
            grid=(nb, nf),
            in_specs=[pl.BlockSpec((1, 1, tm), lambda i, j, be, nv: (i, 0, 0), memory_space=pltpu.SMEM),
                      pl.BlockSpec((1, 1, tm), lambda i, j, be, nv: (jnp.minimum(i + 1, nb - 1), 0, 0),
                                   memory_space=pltpu.SMEM),
                      pl.BlockSpec(memory_space=pl.ANY),
                      pl.BlockSpec((1, d, tf), wcol), pl.BlockSpec((1, d, tf), wcol), pl.BlockSpec((1, tf, d), wrow)],
            out_specs=pl.BlockSpec((tm, d), lambda i, j, be, nv: (i, 0)),
            scratch_shapes=[pltpu.VMEM((2, tm, d), hn.dtype), pltpu.VMEM((tm, d), BF16),
                            pltpu.SemaphoreType.DMA((2,))],
        ),
        out_shape=jax.ShapeDtypeStruct((nb * tm, d), F32),
        compiler_params=_params(("arbitrary", "arbitrary")),
        name="ffn_moe",
    )(block_e, n_valid, src3, src3, hn, wg, wu, wd)


def _conv_silu(tail_ref, x_ref, cw_ref, r0, lo, bias=None, act=True):
    sl = slice(lo, lo + LANES)
    win = jnp.concatenate([tail_ref[:, sl], x_ref[pl.ds(r0, CHUNK), sl]], axis=0)
    w = cw_ref[:, sl]
    y = win * w[CONV_K - 1:CONV_K]
    for k in range(1, CONV_K):
        y = y + pltpu.roll(win, k, 0) * w[CONV_K - 1 - k:CONV_K - k]
    y = y[SUBLANES:]
    if bias is not None:
        y = y + bias
    return _silu(y) if act else y


def _rms_gate(o, gain, z):
    ms = jnp.mean(o * o, axis=-1, keepdims=True)
    return o * lax.rsqrt(ms + EPS) * gain * _silu(z)


def _gdn_kernel(q_ref, k_ref, v_ref, z_ref, beta_ref, alpha_ref, cw_ref, alog_ref, dtb_ref, na_ref,
                o_ref, s_ref, tq_ref, tk_ref, tv_ref, *, n_heads):
    C = CHUNK
    tb = q_ref.shape[0]

    @pl.when(pl.program_id(1) == 0)
    def _():
        s_ref[...] = jnp.zeros_like(s_ref)
        tq_ref[...] = jnp.zeros_like(tq_ref)
        tk_ref[...] = jnp.zeros_like(tk_ref)
        tv_ref[...] = jnp.zeros_like(tv_ref)

    row = lax.broadcasted_iota(jnp.int32, (C, C), 0)
    col = lax.broadcasted_iota(jnp.int32, (C, C), 1)
    causal = row >= col
    strict = row > col
    eye = (row == col).astype(F32)
    neg_a = -jnp.exp(alog_ref[...])
    dtb = dtb_ref[...]
    gain = na_ref[...]
    scale = float(LANES) ** -0.5

    def chunk(c, carry):
        r0 = pl.multiple_of(c * C, C)
        beta_c = jax.nn.sigmoid(beta_ref[pl.ds(r0, C), :])
        g_c = neg_a * _softplus(alpha_ref[pl.ds(r0, C), :] + dtb)
        gam_c = _cumsum_rows(g_c)
        gam_t = gam_c.T
        heads = range(n_heads)
        low, att, rhs, wqg, kg, eglast = [], [], [], [], [], []
        for h in heads:
            lo = h * LANES
            q = _conv_silu(tq_ref, q_ref, cw_ref.at[0], r0, lo)
            k = _conv_silu(tk_ref, k_ref, cw_ref.at[1], r0, lo)
            v = _conv_silu(tv_ref, v_ref, cw_ref.at[2], r0, lo)
            q = q * lax.rsqrt(jnp.sum(q * q, axis=-1, keepdims=True) + EPS) * scale
            k = k * lax.rsqrt(jnp.sum(k * k, axis=-1, keepdims=True) + EPS)
            gcol = gam_c[:, h:h + 1]
            grow = gam_t[h:h + 1, :]
            bcol = beta_c[:, h:h + 1]
            dec = jnp.where(causal, jnp.exp(jnp.where(causal, gcol - grow, 0.0)), 0.0)
            kb = k * bcol
            m1 = _dot_nt(jnp.concatenate([kb, q], axis=0).astype(BF16), k.astype(BF16))
            low.append(jnp.where(strict, m1[:C] * dec, 0.0))
            att.append((m1[C:] * dec).astype(BF16))
            egam = jnp.exp(gcol)
            glast = gam_c[C - 1:C, h:h + 1]
            rhs.append(jnp.concatenate([v * bcol, kb * egam], axis=1).astype(BF16))
            wqg.append((q * egam).astype(BF16))
            kg.append((k * jnp.exp(glast - gcol)).astype(BF16))
            eglast.append(jnp.exp(glast))
        tinv = [eye - low[h] for h in heads]
        lp = [low[h].astype(BF16) for h in heads]
        for it in range(int(math.log2(C)) - 1):
            lp = [_dot(lp[h], lp[h]).astype(BF16) for h in heads]
            tinv = [tinv[h] + _dot(tinv[h].astype(BF16), lp[h]) for h in heads]
        sol = [_dot(tinv[h].astype(BF16), rhs[h]) for h in heads]
        s_old = [s_ref[h] for h in heads]
        wq = [_dot(jnp.concatenate([sol[h][:, LANES:].astype(BF16), wqg[h]], axis=0), s_old[h].astype(BF16))
              for h in heads]
        vnb = [(sol[h][:, :LANES] - wq[h][:C]).astype(BF16) for h in heads]
        for h in heads:
            s_ref[h] = eglast[h] * s_old[h] + _dot_tn(kg[h], vnb[h])
        for h in heads:
            lo = h * LANES
            o = wq[h][C:] + _dot(att[h], vnb[h])
            z = z_ref[pl.ds(r0, C), lo:lo + LANES]
            o_ref[pl.ds(r0, C), lo:lo + LANES] = _rms_gate(o, gain, z).astype(o_ref.dtype)
        tq_ref[...] = q_ref[pl.ds(r0 + C - SUBLANES, SUBLANES), :]
        tk_ref[...] = k_ref[pl.ds(r0 + C - SUBLANES, SUBLANES), :]
        tv_ref[...] = v_ref[pl.ds(r0 + C - SUBLANES, SUBLANES), :]
        return carry

    lax.fori_loop(0, tb // C, chunk, 0)


def _gdn(proj, conv_w, a_log, dt_bias, norm_a, bsz, seq, wa, tb=256):
    n_heads = wa // LANES
    nt = seq // tb
    gate_blk = 8 * wa // LANES

    def colblk(cb):
        return pl.BlockSpec((tb, wa), lambda b, t: (b * nt + t, cb))

    def pad_row(p):
        return jnp.zeros((1, LANES), F32).at[0, :n_heads].set(p.astype(F32))

    small = lambda shape: pl.BlockSpec(shape, lambda b, t: (0,) * len(shape))
    return pl.pallas_call(
        functools.partial(_gdn_kernel, n_heads=n_heads),
        grid=(bsz, nt),
        in_specs=[colblk(0), colblk(1), colblk(2), colblk(3),
                  pl.BlockSpec((tb, LANES), lambda b, t: (b * nt + t, gate_blk)),
                  pl.BlockSpec((tb, LANES), lambda b, t: (b * nt + t, gate_blk + 1)),
                  small((3, CONV_K, wa)), small((1, LANES)), small((1, LANES)), small((1, LANES))],
        out_specs=pl.BlockSpec((tb, wa), lambda b, t: (b * nt + t, 0)),
        out_shape=jax.ShapeDtypeStruct((bsz * seq, wa), BF16),
        scratch_shapes=[pltpu.VMEM((n_heads, LANES, LANES), F32),
                        pltpu.VMEM((SUBLANES, wa), F32), pltpu.VMEM((SUBLANES, wa), F32),
                        pltpu.VMEM((SUBLANES, wa), F32)],
        compiler_params=_params(("parallel", "arbitrary")),
        name="gdn",
    )(proj, proj, proj, proj, proj, proj,
      conv_w.astype(F32).reshape(CONV_K, 3, wa).transpose(1, 0, 2),
      pad_row(a_log), pad_row(dt_bias), norm_a.astype(F32).reshape(1, LANES))


def _ret_kernel(q_ref, k_ref, v_ref, g_ref, cos_ref, sin_ref, nb_ref, o_ref, r_ref, dm_ref, xi_ref, zeta_ref,
                *, n_heads):
    C = CHUNK
    tb = q_ref.shape[0]
    log_gammas = [math.log1p(-(2.0 ** (-5.0 - h))) for h in range(n_heads)]

    @pl.when(pl.program_id(1) == 0)
    def _():
        r_ref[...] = jnp.zeros_like(r_ref)
        row = lax.broadcasted_iota(jnp.int32, (C, C), 0)
        col = lax.broadcasted_iota(jnp.int32, (C, C), 1)
        rel = (row - col).astype(F32)
        idx = lax.broadcasted_iota(jnp.int32, (C, LANES), 0).astype(F32)
        for h in range(n_heads):
            dm_ref[h] = jnp.where(rel >= 0, jnp.exp(jnp.maximum(rel, 0.0) * log_gammas[h]), 0.0)
            xi_ref[h] = jnp.exp((idx + 1.0) * log_gammas[h])
            zeta_ref[h] = jnp.exp((C - 1.0 - idx) * log_gammas[h])

    scale = float(LANES) ** -0.5

    def chunk(c, carry):
        r0 = pl.multiple_of(c * C, C)
        cos = cos_ref[pl.ds(r0, C), :]
        sin = sin_ref[pl.ds(r0, C), :]
        for h in range(n_heads):
            lo = h * LANES
            sl = slice(lo, lo + LANES)
            g_chunk = math.exp(C * log_gammas[h])
            q = q_ref[pl.ds(r0, C), sl]
            k = k_ref[pl.ds(r0, C), sl]
            v = v_ref[pl.ds(r0, C), sl]
            q = q * cos + pltpu.roll(q, LANES // 2, 1) * sin
            k = (k * cos + pltpu.roll(k, LANES // 2, 1) * sin) * scale
            kb = k.astype(BF16)
            vb = v.astype(BF16)
            scores = _dot_nt(q.astype(BF16), kb) * dm_ref[h]
            r_h = r_ref[h]
            o = _dot(scores.astype(BF16), vb) + _dot((q * xi_ref[h]).astype(BF16), r_h.astype(BF16))
            r_ref[h] = g_chunk * r_h + _dot_tn((k * zeta_ref[h]).astype(BF16), vb)
            mu = jnp.mean(o, axis=-1, keepdims=True)
            oc = o - mu
            var = jnp.mean(oc * oc, axis=-1, keepdims=True)
            g = g_ref[pl.ds(r0, C), sl]
            o_ref[pl.ds(r0, C), sl] = (oc * lax.rsqrt(var + EPS) * nb_ref[:, sl] * _silu(g)).astype(o_ref.dtype)
        return carry

    lax.fori_loop(0, tb // C, chunk, 0)


def _retention(proj, norm_b, bsz, seq, wb, col0, tb=256):
    n_heads = wb // LANES
    nt = seq // tb
    half = LANES // 2
    pos = jnp.arange(seq, dtype=F32)
    theta = 1.0 / ROPE_BASE ** jnp.linspace(0.0, 1.0, half, dtype=F32)
    ang = pos[:, None] * theta[None, :]
    cos, sin = jnp.cos(ang), jnp.sin(ang)
    cos2 = jnp.concatenate([cos, cos], axis=-1)
    sin2 = jnp.concatenate([-sin, sin], axis=-1)

    def colblk(cb):
        return pl.BlockSpec((tb, wb), lambda b, t: (b * nt + t, col0 + cb))

    return pl.pallas_call(
        functools.partial(_ret_kernel, n_heads=n_heads),
        grid=(bsz, nt),
        in_specs=[colblk(0), colblk(1), colblk(2), colblk(3),
                  pl.BlockSpec((tb, LANES), lambda b, t: (t, 0)),
                  pl.BlockSpec((tb, LANES), lambda b, t: (t, 0)),
                  pl.BlockSpec((1, wb), lambda b, t: (0, 0))],
        out_specs=pl.BlockSpec((tb, wb), lambda b, t: (b * nt + t, 0)),
        out_shape=jax.ShapeDtypeStruct((bsz * seq, wb), BF16),
        scratch_shapes=[pltpu.VMEM((n_heads, LANES, LANES), F32), pltpu.VMEM((n_heads, CHUNK, CHUNK), F32),
                        pltpu.VMEM((n_heads, CHUNK, LANES), F32), pltpu.VMEM((n_heads, CHUNK, LANES), F32)],
        compiler_params=_params(("parallel", "arbitrary")),
        name="retention",
    )(proj, proj, proj, proj, cos2, sin2, norm_b.astype(F32).reshape(1, wb))


def _rglru_kernel(y_ref, x_ref, cw_ref, cb_ref, wa_ref, ba_ref, wx_ref, bx_ref, lam_ref,
                  o_ref, tail_ref, h_ref, *, n_groups):
    C = CHUNK
    tb = x_ref.shape[0]

    @pl.when(pl.program_id(1) == 0)
    def _():
        tail_ref[...] = jnp.zeros_like(tail_ref)
        h_ref[...] = jnp.zeros_like(h_ref)

    rows = lax.broadcasted_iota(jnp.int32, (C, LANES), 0)

    def chunk(c, carry):
        r0 = pl.multiple_of(c * C, C)
        for g in range(n_groups):
            lo = g * LANES
            sl = slice(lo, lo + LANES)
            u = _conv_silu(tail_ref, x_ref, cw_ref, r0, lo, bias=cb_ref[:, sl], act=False)
            ub = u.astype(BF16)
            r_gate = jax.nn.sigmoid(_dot(ub, wa_ref[g]) + ba_ref[:, sl])
            i_gate = jax.nn.sigmoid(_dot(ub, wx_ref[g]) + bx_ref[:, sl])
            log_a = -RG_C * r_gate * _softplus(-lam_ref[:, sl])
            a = jnp.exp(log_a)
            th = jnp.tanh(log_a)
            b = jnp.sqrt(-2.0 * th / (1.0 - th)) * (i_gate * u)
            d = 1
            while d < C:
                keep = rows >= d
                b = b + a * jnp.where(keep, pltpu.roll(b, d, 0), 0.0)
                a = a * jnp.where(keep, pltpu.roll(a, d, 0), 1.0)
                d *= 2
            h = b + a * h_ref[:, sl]
            h_ref[:, sl] = h[C - 1:C]
            y = y_ref[pl.ds(r0, C), sl]
            o_ref[pl.ds(r0, C), sl] = (jax.nn.gelu(y, approximate=True) * h).astype(o_ref.dtype)
        tail_ref[...] = x_ref[pl.ds(r0 + C - SUBLANES, SUBLANES), :]
        return carry

    lax.fori_loop(0, tb // C, chunk, 0)


def _rglru(proj, conv_w, conv_b, rg_wa, rg_ba, rg_wx, rg_bx, rg_lambda, bsz, seq, wc, tb=256):
    n_groups = wc // LANES
    nt = seq // tb
    row = lambda p: p.astype(F32).reshape(1, wc)
    small = lambda shape: pl.BlockSpec(shape, lambda b, t: (0,) * len(shape))
    return pl.pallas_call(
        functools.partial(_rglru_kernel, n_groups=n_groups),
        grid=(bsz, nt),
        in_specs=[pl.BlockSpec((tb, wc), lambda b, t: (b * nt + t, 0)),
                  pl.BlockSpec((tb, wc), lambda b, t: (b * nt + t, 1)),
                  small((CONV_K, wc)), small((1, wc)),
                  small((n_groups, LANES, LANES)), small((1, wc)),
                  small((n_groups, LANES, LANES)), small((1, wc)), small((1, wc))],
        out_specs=pl.BlockSpec((tb, wc), lambda b, t: (b * nt + t, 0)),
        out_shape=jax.ShapeDtypeStruct((bsz * seq, wc), BF16),
        scratch_shapes=[pltpu.VMEM((SUBLANES, wc), F32), pltpu.VMEM((1, wc), F32)],
        compiler_params=_params(("parallel", "arbitrary")),
        name="rglru",
    )(proj, proj, conv_w.astype(F32), row(conv_b), rg_wa.astype(BF16), row(rg_ba),
      rg_wx.astype(BF16), row(rg_bx), row(rg_lambda))


def _hgrn2_kernel(q_ref, f_ref, i_ref, g_ref, lb_ref, nd_ref, o_ref, st_ref, gam_ref, mask_ref, *, n_heads):
    C = CHUNK
    tb = q_ref.shape[0]
    levels = [C >> (l + 1) for l in range(int(math.log2(C)))]

    @pl.when(pl.program_id(1) == 0)
    def _():
        st_ref[...] = jnp.zeros_like(st_ref)
        row = lax.broadcasted_iota(jnp.int32, (C, C), 0)
        col = lax.broadcasted_iota(jnp.int32, (C, C), 1)
        mask_ref[0] = (row == col).astype(F32)
        for l, b in enumerate(levels):
            pair = ((row & -(2 * b)) == (col & -(2 * b))) & ((row & b) != 0) & ((col & b) == 0)
            mask_ref[l + 1] = pair.astype(F32)

    rows = lax.broadcasted_iota(jnp.int32, (C, LANES), 0)
    gain = nd_ref[...]

    def chunk(c, carry):
        r0 = pl.multiple_of(c * C, C)
        for h in range(n_heads):
            lo = h * LANES
            sl = slice(lo, lo + LANES)
            lb = lb_ref[:, sl]
            f_lin = f_ref[pl.ds(r0, C), sl]
            log_sig = jnp.minimum(f_lin, 0.0) - jnp.log(1.0 + jnp.exp(-jnp.abs(f_lin)))
            la, lc = jnp.log(lb), jnp.log1p(-lb) + log_sig
            log_f = jnp.maximum(la, lc) + jnp.log(1.0 + jnp.exp(-jnp.abs(la - lc)))
            k = (1.0 - lb) * jax.nn.sigmoid(-f_lin)
            q = _silu(q_ref[pl.ds(r0, C), sl])
            v = i_ref[pl.ds(r0, C), sl]
            vb = v.astype(BF16)
            gam = _cumsum_rows(log_f)
            gam_ref[...] = gam
            att = mask_ref[0] * _dot_nt(q.astype(BF16), k.astype(BF16))
            for l, b in enumerate(levels):
                nblk = C // (2 * b)
                if b >= SUBLANES // 2:
                    parts = [jnp.broadcast_to(gam_ref[pl.ds(j * 2 * b + b - 1, 1), :], (2 * b, LANES)) for j in range(nblk)]
                    gmid = jnp.concatenate(parts, axis=0) if nblk > 1 else parts[0]
                else:
                    off = (rows & (2 * b - 1)) - (b - 1)
                    gmid = jnp.zeros_like(gam)
                    for o_ in range(-(b - 1), b + 1):
                        gmid = jnp.where(off == o_, pltpu.roll(gam, o_ % C, 0), gmid)
                t = (jnp.where((rows & b) != 0, q, k) * jnp.exp(-jnp.abs(gam - gmid))).astype(BF16)
                att = att + mask_ref[l + 1] * _dot_nt(t, t)
            st = st_ref[h]
            o = _dot(att.astype(BF16), vb) + _dot_nt((q * jnp.exp(gam)).astype(BF16), st.astype(BF16))
            glast = gam[C - 1:C, :]
            kg = (k * jnp.exp(glast - gam)).astype(BF16)
            st_ref[h] = st * jnp.exp(glast) + _dot_tn(vb, kg)
            g = g_ref[pl.ds(r0, C), sl]
            o_ref[pl.ds(r0, C), sl] = _rms_gate(o, gain, g).astype(o_ref.dtype)
        return carry

    lax.fori_loop(0, tb // C, chunk, 0)


def _hgrn2(proj, lb, norm_d, bsz, seq, wd, col0, tb=256):
    n_heads = wd // LANES
    nt = seq // tb

    def colblk(cb):
        return pl.BlockSpec((tb, wd), lambda b, t: (b * nt + t, col0 + cb))

    return pl.pallas_call(
        functools.partial(_hgrn2_kernel, n_heads=n_heads),
        grid=(bsz, nt),
        in_specs=[colblk(0), colblk(1), colblk(2), colblk(3),
                  pl.BlockSpec((1, wd), lambda b, t: (0, 0)), pl.BlockSpec((1, LANES), lambda b, t: (0, 0))],
        out_specs=pl.BlockSpec((tb, wd), lambda b, t: (b * nt + t, 0)),
        out_shape=jax.ShapeDtypeStruct((bsz * seq, wd), BF16),
        scratch_shapes=[pltpu.VMEM((n_heads, LANES, LANES), F32), pltpu.VMEM((CHUNK, LANES), F32),
                        pltpu.VMEM((int(math.log2(CHUNK)) + 1, CHUNK, CHUNK), F32)],
        compiler_params=_params(("parallel", "arbitrary")),
        name="hgrn2",
    )(proj, proj, proj, proj, lb.astype(F32).reshape(1, wd), norm_d.astype(F32).reshape(1, LANES))


def _router_kernel(x_ref, g_ref, wr_ref, hn_ref, r_ref):
    x = x_ref[...]
    ms = jnp.mean(x * x, axis=-1, keepdims=True)
    hn = x * lax.rsqrt(ms + EPS) * g_ref[...]
    hn_ref[...] = hn
    w = wr_ref[...]
    h1 = hn.astype(BF16)
    h2 = (hn - h1.astype(F32)).astype(BF16)
    h3 = (hn - h1.astype(F32) - h2.astype(F32)).astype(BF16)
    w1 = w.astype(BF16)
    w2 = (w - w1.astype(F32)).astype(BF16)
    w3 = (w - w1.astype(F32) - w2.astype(F32)).astype(BF16)
    logits = (_dot(h1, w1) + (_dot(h1, w2) + _dot(h2, w1))
              + (_dot(h2, w2) + _dot(h1, w3) + _dot(h3, w1)))
    lane = lax.broadcasted_iota(jnp.int32, logits.shape, 1)
    lg = jnp.where(lane < N_EXPERTS, logits, -jnp.inf)
    m1 = jnp.max(lg, axis=-1, keepdims=True)
    i1 = jnp.min(jnp.where(lg == m1, lane, LANES), axis=-1, keepdims=True)
    lg2 = jnp.where(lane == i1, -jnp.inf, lg)
    m2 = jnp.max(lg2, axis=-1, keepdims=True)
    i2 = jnp.min(jnp.where(lg2 == m2, lane, LANES), axis=-1, keepdims=True)
    e = jnp.exp(m2 - m1)
    g1 = 1.0 / (1.0 + e)
    g2 = e / (1.0 + e)
    r_ref[...] = jnp.where(lane == 0, i1.astype(F32), jnp.where(lane == 1, i2.astype(F32),
                           jnp.where(lane == 2, g1, jnp.where(lane == 3, g2, 0.0))))


def _router(x, g, w_router, tm=256):
    t, d = x.shape
    wr = jnp.zeros((d, LANES), F32).at[:, :N_EXPERTS].set(w_router.astype(F32))
    return pl.pallas_call(
        _router_kernel,
        grid=(t // tm,),
        in_specs=[pl.BlockSpec((tm, d), lambda i: (i, 0)), pl.BlockSpec((1, d), lambda i: (0, 0)),
                  pl.BlockSpec((d, LANES), lambda i: (0, 0))],
        out_specs=[pl.BlockSpec((tm, d), lambda i: (i, 0)), pl.BlockSpec((tm, LANES), lambda i: (i, 0))],
        out_shape=[jax.ShapeDtypeStruct((t, d), F32), jax.ShapeDtypeStruct((t, LANES), F32)],
        compiler_params=_params(("parallel",)),
        name="router",
    )(x, g.reshape(1, d), wr)


def _combine_kernel(dest_ref, dest_next_ref, r_ref, x_ref, y_ref, gf_ref, o_ref, buf_ref, sems):
    i = pl.program_id(0)
    tb = x_ref.shape[0]
    slot = lax.rem(i, 2)

    @pl.when(i == 0)
    def _():
        _gather_rows(dest_ref, TOP_K * tb, y_ref, buf_ref.at[0], sems.at[0])

    @pl.when(i + 1 < pl.num_programs(0))
    def _():
        _gather_rows(dest_next_ref, TOP_K * tb, y_ref, buf_ref.at[1 - slot], sems.at[1 - slot])

    _wait_rows(TOP_K * tb, y_ref, buf_ref.at[slot], sems.at[slot])
    gates = r_ref[...]
    acc = x_ref[...] + gates[:, 2:3] * buf_ref[slot, :tb] + gates[:, 3:4] * buf_ref[slot, tb:]
    ms = jnp.mean(acc * acc, axis=-1, keepdims=True)
    o_ref[...] = acc * lax.rsqrt(ms + EPS) * gf_ref[...]


def _combine(x, y_pad, dest, route, g_final, tb=256):
    t, d = x.shape
    nb = t // tb
    table = dest.reshape(nb, tb, TOP_K).transpose(0, 2, 1).reshape(nb, 1, TOP_K * tb)
    return pl.pallas_call(
        _combine_kernel,
        grid=(nb,),
        in_specs=[pl.BlockSpec((1, 1, TOP_K * tb), lambda i: (i, 0, 0), memory_space=pltpu.SMEM),
                  pl.BlockSpec((1, 1, TOP_K * tb), lambda i: (jnp.minimum(i + 1, nb - 1), 0, 0),
                               memory_space=pltpu.SMEM),
                  pl.BlockSpec((tb, LANES), lambda i: (i, 0)),
                  pl.BlockSpec((tb, d), lambda i: (i, 0)),
                  pl.BlockSpec(memory_space=pl.ANY),
                  pl.BlockSpec((1, d), lambda i: (0, 0))],
        out_specs=pl.BlockSpec((tb, d), lambda i: (i, 0)),
        out_shape=jax.ShapeDtypeStruct((t, d), F32),
        scratch_shapes=[pltpu.VMEM((2, TOP_K * tb, d), F32), pltpu.SemaphoreType.DMA((2,))],
        compiler_params=_params(("arbitrary",)),
        name="moe_combine",
    )(table, table, route, x, y_pad, g_final.reshape(1, d))


def _routing_tables(route, tm):
    t = route.shape[0]
    flat_e = route[:, :TOP_K].astype(jnp.int32).reshape(-1)
    onehot = (flat_e[:, None] == jnp.arange(N_EXPERTS, dtype=jnp.int32)[None, :]).astype(jnp.int32)
    cum = jnp.cumsum(onehot, axis=0)
    rank = jnp.sum((cum - onehot) * onehot, axis=1)
    counts = cum[-1]
    padded = (counts + tm - 1) // tm * tm
    p_end = jnp.cumsum(padded)
    p_start = p_end - padded
    dest = (jnp.sum(onehot * p_start[None, :], axis=1) + rank).astype(jnp.int32)
    n_blocks = -(-(t * TOP_K) // tm) + N_EXPERTS
    block_e = jnp.minimum(jnp.searchsorted(p_end, jnp.arange(n_blocks, dtype=jnp.int32) * tm, side='right'),
                          N_EXPERTS - 1).astype(jnp.int32)
    n_valid = (p_end[-1] // tm).astype(jnp.int32).reshape(1)
    order = jnp.argsort(flat_e, stable=True).astype(jnp.int32)
    rows = jnp.arange(n_blocks * tm, dtype=jnp.int32)
    row_e = jnp.repeat(block_e, tm)
    row_rank = rows - p_start[row_e]
    entry = order[jnp.clip((jnp.cumsum(counts) - counts)[row_e] + row_rank, 0, t * TOP_K - 1)]
    src = jnp.where(row_rank < counts[row_e], entry // TOP_K, 0).astype(jnp.int32)
    return dest, src, block_e, n_valid


def kernel(x, norm_mix, norm_ffn, w_in_ab, conv_a, a_log, dt_bias, norm_a, norm_b, w_out_ab, w_gate_dense,
           w_up_dense, w_down_dense, w_in_cd, conv_c_w, conv_c_b, rg_wa, rg_ba, rg_wx, rg_bx, rg_lambda, hgrn_lb,
           norm_d, w_out_cd, w_router, w_gate_moe, w_up_moe, w_down_moe, norm_final):
    bsz, seq, d = x.shape
    xf = x.reshape(bsz * seq, d).astype(F32)
    x1 = _mixer_ab(xf, norm_mix[0], w_in_ab[0], conv_a[0], a_log[0], dt_bias[0], norm_a[0], norm_b[0], w_out_ab[0],
                   bsz, seq)
    x2 = _dense_ffn(x1, norm_ffn[0], w_gate_dense.astype(BF16), w_up_dense.astype(BF16), w_down_dense.astype(BF16))
    lb_soft = jax.nn.softmax(hgrn_lb.astype(F32), axis=0)
    lb_all = jnp.cumsum(lb_soft, axis=0) - lb_soft[0:1]
    x3 = _mixer_cd(x2, norm_mix[1], w_in_cd[0], conv_c_w[0], conv_c_b[0], rg_wa[0], rg_ba[0], rg_wx[0], rg_bx[0],
                   rg_lambda[0], lb_all[1], norm_d[0], w_out_cd[0], bsz, seq)
    out = _moe_ffn_final(x3, norm_ffn[1], w_router[0], w_gate_moe[0], w_up_moe[0], w_down_moe[0], norm_final)
    return out.reshape(bsz, seq, d).astype(x.dtype)


def _mixer_ab(xf, g_mix, w_in, conv_a, a_log, dt_bias, norm_a, norm_b, w_out, bsz, seq):
    wa = xf.shape[1] // 2
    n_ha = wa // LANES
    pad = lambda cols: jnp.pad(cols, ((0, 0), (0, LANES - cols.shape[1])))
    w_perm = jnp.concatenate([w_in[:, :4 * wa], w_in[:, 4 * wa + 2 * n_ha:],
                              pad(w_in[:, 4 * wa:4 * wa + n_ha]), pad(w_in[:, 4 * wa + n_ha:4 * wa + 2 * n_ha])],
                             axis=1).astype(BF16)
    proj = _rms_matmul(xf, g_mix, w_perm, name="in_proj_ab")
    o_a = _gdn(proj, conv_a, a_log, dt_bias, norm_a, bsz, seq, wa)
    o_b = _retention(proj, norm_b, bsz, seq, wa, col0=4)
    wo = w_out.astype(BF16)
    return _matmul([o_a, o_b], [wo[:wa], wo[wa:]], res=xf, tm=512, tn=1024, name="out_proj_ab")


def _mixer_cd(x2, g_mix, w_in, conv_w, conv_b, rg_wa, rg_ba, rg_wx, rg_bx, rg_lambda, lb, norm_d, w_out, bsz, seq):
    wc = x2.shape[1] // 2
    proj = _rms_matmul(x2, g_mix, w_in.astype(BF16), name="in_proj_cd")
    o_c = _rglru(proj, conv_w, conv_b, rg_wa, rg_ba, rg_wx, rg_bx, rg_lambda, bsz, seq, wc)
    o_d = _hgrn2(proj, lb, norm_d, bsz, seq, wc, col0=2)
    wo = w_out.astype(BF16)
    return _matmul([o_c, o_d], [wo[:wc], wo[wc:]], res=x2, tm=512, tn=1024, name="out_proj_cd")


def _moe_ffn_final(x3, g_ffn, w_router, w_gate, w_up, w_down, g_final, tm=512):
    hn_moe, route = _router(x3, g_ffn, w_router)
    dest, src, block_e, n_valid = _routing_tables(route, tm)
    y_pad = _moe_ffn_call(hn_moe, src, w_gate.astype(BF16), w_up.astype(BF16), w_down.astype(BF16), block_e, n_valid,
                          tm=tm)
    return _combine(x3, y_pad, dest, route, g_final)
```

```python
import functools
import math

import jax
import jax.numpy as jnp
from jax import lax
from jax.experimental import pallas as pl
from jax.experimental.pallas import tpu as pltpu

F32 = jnp.float32
BF16 = jnp.bfloat16
LANES = 128
SUBLANES = 8
EPS = 1e-6
CONV_K = 4
RG_C = 8.0
ROPE_BASE = 10000.0
N_EXPERTS = 8
TOP_K = 2
CHUNK = 128
MIXER_TB = 512
VMEM_LIMIT = 48 * 2**20


def _params(sem):
    return pltpu.CompilerParams(dimension_semantics=sem, vmem_limit_bytes=VMEM_LIMIT)


def _dot(a, b):
    return jnp.dot(a, b, preferred_element_type=F32)


def _dot_nt(a, b):
    return lax.dot_general(a, b, (((1,), (1,)), ((), ())), preferred_element_type=F32)


def _dot_tn(a, b):
    return lax.dot_general(a, b, (((0,), (0,)), ((), ())), preferred_element_type=F32)


def _softplus(x):
    return jnp.maximum(x, 0.0) + jnp.log1p(jnp.exp(-jnp.abs(x)))


def _silu(x):
    return x * jax.nn.sigmoid(x)


def _cumsum_rows(x):
    n = x.shape[0]
    rows = lax.broadcasted_iota(jnp.int32, x.shape, 0)
    d = 1
    while d < n:
        x = x + jnp.where(rows >= d, pltpu.roll(x, d, 0), 0.0)
        d *= 2
    return x


def _rms(x, g):
    ms = jnp.mean(x * x, axis=-1, keepdims=True)
    return x * lax.rsqrt(ms + EPS) * g


def _rms_mm_kernel(x_ref, g_ref, w_ref, o_ref, hn_ref):
    @pl.when(pl.program_id(1) == 0)
    def _():
        hn_ref[...] = _rms(x_ref[...], g_ref[...]).astype(hn_ref.dtype)

    o_ref[...] = _dot(hn_ref[...], w_ref[...])


def _rms_matmul(x, g, w, tm=1024, tn=768, name="rms_matmul"):
    m, k = x.shape
    n = w.shape[1]
    tm = min(tm, m)
    return pl.pallas_call(
        _rms_mm_kernel,
        grid=(m // tm, n // tn),
        in_specs=[pl.BlockSpec((tm, k), lambda i, j: (i, 0)), pl.BlockSpec((1, k), lambda i, j: (0, 0)),
                  pl.BlockSpec((k, tn), lambda i, j: (0, j))],
        out_specs=pl.BlockSpec((tm, tn), lambda i, j: (i, j)),
        out_shape=jax.ShapeDtypeStruct((m, n), F32),
        scratch_shapes=[pltpu.VMEM((tm, k), BF16)],
        compiler_params=_params(("parallel", "arbitrary")),
        name=name,
    )(x, g.astype(F32).reshape(1, k), w)


def _mm_kernel(*refs, n_a, has_res):
    o_ref = refs[-1]
    acc = _dot(refs[0][...], refs[n_a][...])
    for t in range(1, n_a):
        acc = acc + _dot(refs[t][...], refs[n_a + t][...])
    if has_res:
        acc = acc + refs[2 * n_a][...]
    o_ref[...] = acc.astype(o_ref.dtype)


def _matmul(a_list, w_list, res=None, out_dtype=F32, tm=1024, tn=512, name="matmul"):
    m = a_list[0].shape[0]
    n = w_list[0].shape[1]
    tm, tn = min(tm, m), min(tn, n)
    in_specs = [pl.BlockSpec((tm, a.shape[1]), lambda i, j: (i, 0)) for a in a_list]
    in_specs += [pl.BlockSpec((w.shape[0], tn), lambda i, j: (0, j)) for w in w_list]
    args = list(a_list) + list(w_list)
    if res is not None:
        in_specs.append(pl.BlockSpec((tm, tn), lambda i, j: (i, j)))
        args.append(res)
    return pl.pallas_call(
        functools.partial(_mm_kernel, n_a=len(a_list), has_res=res is not None),
        grid=(m // tm, n // tn),
        in_specs=in_specs,
        out_specs=pl.BlockSpec((tm, tn), lambda i, j: (i, j)),
        out_shape=jax.ShapeDtypeStruct((m, n), out_dtype),
        compiler_params=_params(("parallel", "arbitrary")),
        name=name,
    )(*args)


def _swiglu_accumulate(x, wg_ref, wu_ref, wd_ref, o_ref):
    a = _silu(_dot(x, wg_ref[0])) * _dot(x, wu_ref[0])
    o_ref[...] += _dot(a.astype(BF16), wd_ref[0])


def _dense_ffn_kernel(x_ref, g_ref, wg_ref, wu_ref, wd_ref, o_ref, hn_ref):
    @pl.when(pl.program_id(1) == 0)
    def _():
        x = x_ref[...]
        o_ref[...] = x
        hn_ref[...] = _rms(x, g_ref[...]).astype(hn_ref.dtype)

    _swiglu_accumulate(hn_ref[...], wg_ref, wu_ref, wd_ref, o_ref)


def _dense_ffn(x, g, wg, wu, wd, tm=512, tf=512):
    m, d = x.shape
    f = wg.shape[2]
    return pl.pallas_call(
        _dense_ffn_kernel,
        grid=(m // tm, f // tf),
        in_specs=[pl.BlockSpec((tm, d), lambda i, j: (i, 0)),
                  pl.BlockSpec((1, d), lambda i, j: (0, 0)),
                  pl.BlockSpec((1, d, tf), lambda i, j: (0, 0, j)),
                  pl.BlockSpec((1, d, tf), lambda i, j: (0, 0, j)),
                  pl.BlockSpec((1, tf, d), lambda i, j: (0, j, 0))],
        out_specs=pl.BlockSpec((tm, d), lambda i, j: (i, 0)),
        out_shape=jax.ShapeDtypeStruct((m, d), F32),
        scratch_shapes=[pltpu.VMEM((tm, d), BF16)],
        compiler_params=_params(("parallel", "arbitrary")),
        name="ffn_dense",
    )(x, g.astype(F32).reshape(1, d), wg, wu, wd)


def _row_copy(src, s, dst, d, sem):
    return pltpu.make_async_copy(src.at[pl.ds(s, 1)], dst.at[pl.ds(d, 1)], sem)


DMA_UNROLL = 8


def _gather_rows(table_ref, n_rows, src, dst, sem):
    def body(r, carry):
        _row_copy(src, table_ref[0, 0, r], dst, r, sem).start()
        return carry
    lax.fori_loop(0, n_rows, body, 0, unroll=DMA_UNROLL)


def _wait_rows(n_rows, src, dst, sem):
    def body(r, carry):
        _row_copy(src, 0, dst, 0, sem).wait()
        return carry
    lax.fori_loop(0, n_rows, body, 0, unroll=DMA_UNROLL)


def _moe_ffn_kernel(be_ref, nv_ref, src_ref, src_next_ref, hn_ref, wg_ref, wu_ref, wd_ref, o_ref,
                    rows_ref, x_ref, sem, *, rows_per_step):
    i, j = pl.program_id(0), pl.program_id(1)
    n_valid = nv_ref[0]
    tm = x_ref.shape[0]
    n_rows = rows_ref.shape[0]

    @pl.when(j == 0)
    def _():
        o_ref[...] = jnp.zeros_like(o_ref)

    @pl.when(jnp.logical_and(i <= n_valid, j == 0))
    def _():
        @pl.when(i == 0)
        def _():
            _gather_rows(src_ref, n_rows, hn_ref, rows_ref, sem)

        _wait_rows(n_rows, hn_ref, rows_ref, sem)
        x_ref[...] = rows_ref[:tm].astype(BF16)

    @pl.when(i < n_valid)
    def _():
        for q in range(rows_per_step):
            r = j * rows_per_step + q
            _row_copy(hn_ref, src_next_ref[0, 0, r], rows_ref, r, sem).start()
        _swiglu_accumulate(x_ref[...], wg_ref, wu_ref, wd_ref, o_ref)


def _moe_ffn_call(hn, src, wg, wu, wd, block_e, n_valid, tm, tf=512):
    d = hn.shape[1]
    f = wg.shape[2]
    nb, nf = src.shape[0] // tm, f // tf
    rows_per_step = -(-tm // (nf * SUBLANES)) * SUBLANES
    n_rows = rows_per_step * nf
    src3 = jnp.pad(src.reshape(nb, 1, tm), ((0, 0), (0, 0), (0, n_rows - tm)))

    def wcol(i, j, be, nv):
        return (be[i], 0, jnp.where(i < nv[0], j, nf - 1))

    def wrow(i, j, be, nv):
        return (be[i], jnp.where(i < nv[0], j, nf - 1), 0)

    return pl.pallas_call(
        functools.partial(_moe_ffn_kernel, rows_per_step=rows_per_step),
        grid_spec=pltpu.PrefetchScalarGridSpec(
            num_scalar_prefetch=2,
            grid=(nb, nf),
            in_specs=[pl.BlockSpec((1, 1, n_rows), lambda i, j, be, nv: (i, 0, 0), memory_space=pltpu.SMEM),
                      pl.BlockSpec((1, 1, n_rows), lambda i, j, be, nv: (jnp.minimum(i + 1, nb - 1), 0, 0),
                                   memory_space=pltpu.SMEM),
                      pl.BlockSpec(memory_space=pl.ANY),
                      pl.BlockSpec((1, d, tf), wcol), pl.BlockSpec((1, d, tf), wcol), pl.BlockSpec((1, tf, d), wrow)],
            out_specs=pl.BlockSpec((tm, d), lambda i, j, be, nv: (i, 0)),
            scratch_shapes=[pltpu.VMEM((n_rows, d), hn.dtype), pltpu.VMEM((tm, d), BF16),
                            pltpu.SemaphoreType.DMA(())],
        ),
        out_shape=jax.ShapeDtypeStruct((nb * tm, d), F32),
        compiler_params=_params(("arbitrary", "arbitrary")),
        name="ffn_moe",
    )(block_e, n_valid, src3, src3, hn, wg, wu, wd)


def _conv_silu(tail_ref, x_ref, cw_ref, r0, lo, bias=None, act=True):
    sl = slice(lo, lo + LANES)
    win = jnp.concatenate([tail_ref[:, sl], x_ref[pl.ds(r0, CHUNK), sl]], axis=0)
    w = cw_ref[:, sl]
    y = win * w[CONV_K - 1:CONV_K]
    for k in range(1, CONV_K):
        y = y + pltpu.roll(win, k, 0) * w[CONV_K - 1 - k:CONV_K - k]
    y = y[SUBLANES:]
    if bias is not None:
        y = y + bias
    return _silu(y) if act else y


def _rms_gate(o, gain, z):
    ms = jnp.mean(o * o, axis=-1, keepdims=True)
    return o * lax.rsqrt(ms + EPS) * gain * _silu(z)


def _gdn_kernel(q_ref, k_ref, v_ref, z_ref, beta_ref, alpha_ref, cw_ref, alog_ref, dtb_ref, na_ref,
                o_ref, s_ref, tq_ref, tk_ref, tv_ref, *, n_heads):
    C = CHUNK
    tb = q_ref.shape[0]

    @pl.when(pl.program_id(1) == 0)
    def _():
        s_ref[...] = jnp.zeros_like(s_ref)
        tq_ref[...] = jnp.zeros_like(tq_ref)
        tk_ref[...] = jnp.zeros_like(tk_ref)
        tv_ref[...] = jnp.zeros_like(tv_ref)

    row = lax.broadcasted_iota(jnp.int32, (C, C), 0)
    col = lax.broadcasted_iota(jnp.int32, (C, C), 1)
    causal = row >= col
    strict = row > col
    eye = (row == col).astype(F32)
    neg_a = -jnp.exp(alog_ref[...])
    dtb = dtb_ref[...]
    gain = na_ref[...]
    scale = float(LANES) ** -0.5

    def chunk(c, carry):
        r0 = pl.multiple_of(c * C, C)
        beta_c = jax.nn.sigmoid(beta_ref[pl.ds(r0, C), :])
        g_c = neg_a * _softplus(alpha_ref[pl.ds(r0, C), :] + dtb)
        gam_c = _cumsum_rows(g_c)
        gam_t = gam_c.T
        heads = range(n_heads)
        low, att, rhs, wqg, kg, eglast = [], [], [], [], [], []
        for h in heads:
            lo = h * LANES
            q = _conv_silu(tq_ref, q_ref, cw_ref.at[0], r0, lo)
            k = _conv_silu(tk_ref, k_ref, cw_ref.at[1], r0, lo)
            v = _conv_silu(tv_ref, v_ref, cw_ref.at[2], r0, lo)
            q = q * lax.rsqrt(jnp.sum(q * q, axis=-1, keepdims=True) + EPS) * scale
            k = k * lax.rsqrt(jnp.sum(k * k, axis=-1, keepdims=True) + EPS)
            gcol = gam_c[:, h:h + 1]
            grow = gam_t[h:h + 1, :]
            bcol = beta_c[:, h:h + 1]
            dec = jnp.where(causal, jnp.exp(jnp.where(causal, gcol - grow, 0.0)), 0.0)
            kb = k * bcol
            m1 = _dot_nt(jnp.concatenate([kb, q], axis=0).astype(BF16), k.astype(BF16))
            low.append(jnp.where(strict, m1[:C] * dec, 0.0))
            att.append((m1[C:] * dec).astype(BF16))
            egam = jnp.exp(gcol)
            glast = gam_c[C - 1:C, h:h + 1]
            rhs.append(jnp.concatenate([v * bcol, kb * egam], axis=1).astype(BF16))
            wqg.append((q * egam).astype(BF16))
            kg.append((k * jnp.exp(glast - gcol)).astype(BF16))
            eglast.append(jnp.exp(glast))
        tinv = [eye - low[h] for h in heads]
        lp = [low[h].astype(BF16) for h in heads]
        for it in range(int(math.log2(C)) - 1):
            lp = [_dot(lp[h], lp[h]).astype(BF16) for h in heads]
            tinv = [tinv[h] + _dot(tinv[h].astype(BF16), lp[h]) for h in heads]
        sol = [_dot(tinv[h].astype(BF16), rhs[h]) for h in heads]
        s_old = [s_ref[h] for h in heads]
        wq = [_dot(jnp.concatenate([sol[h][:, LANES:].astype(BF16), wqg[h]], axis=0), s_old[h].astype(BF16))
              for h in heads]
        vnb = [(sol[h][:, :LANES] - wq[h][:C]).astype(BF16) for h in heads]
        for h in heads:
            s_ref[h] = eglast[h] * s_old[h] + _dot_tn(kg[h], vnb[h])
        for h in heads:
            lo = h * LANES
            o = wq[h][C:] + _dot(att[h], vnb[h])
            z = z_ref[pl.ds(r0, C), lo:lo + LANES]
            o_ref[pl.ds(r0, C), lo:lo + LANES] = _rms_gate(o, gain, z).astype(o_ref.dtype)
        tq_ref[...] = q_ref[pl.ds(r0 + C - SUBLANES, SUBLANES), :]
        tk_ref[...] = k_ref[pl.ds(r0 + C - SUBLANES, SUBLANES), :]
        tv_ref[...] = v_ref[pl.ds(r0 + C - SUBLANES, SUBLANES), :]
        return carry

    lax.fori_loop(0, tb // C, chunk, 0)


def _gdn(proj, conv_w, a_log, dt_bias, norm_a, bsz, seq, wa, tb=MIXER_TB):
    n_heads = wa // LANES
    nt = seq // tb
    gate_blk = 8 * wa // LANES

    def colblk(cb):
        return pl.BlockSpec((tb, wa), lambda b, t: (b * nt + t, cb))

    def pad_row(p):
        return jnp.zeros((1, LANES), F32).at[0, :n_heads].set(p.astype(F32))

    small = lambda shape: pl.BlockSpec(shape, lambda b, t: (0,) * len(shape))
    return pl.pallas_call(
        functools.partial(_gdn_kernel, n_heads=n_heads),
        grid=(bsz, nt),
        in_specs=[colblk(0), colblk(1), colblk(2), colblk(3),
                  pl.BlockSpec((tb, LANES), lambda b, t: (b * nt + t, gate_blk)),
                  pl.BlockSpec((tb, LANES), lambda b, t: (b * nt + t, gate_blk + 1)),
                  small((3, CONV_K, wa)), small((1, LANES)), small((1, LANES)), small((1, LANES))],
        out_specs=pl.BlockSpec((tb, wa), lambda b, t: (b * nt + t, 0)),
        out_shape=jax.ShapeDtypeStruct((bsz * seq, wa), BF16),
        scratch_shapes=[pltpu.VMEM((n_heads, LANES, LANES), F32),
                        pltpu.VMEM((SUBLANES, wa), F32), pltpu.VMEM((SUBLANES, wa), F32),
                        pltpu.VMEM((SUBLANES, wa), F32)],
        compiler_params=_params(("parallel", "arbitrary")),
        name="gdn",
    )(proj, proj, proj, proj, proj, proj,
      conv_w.astype(F32).reshape(CONV_K, 3, wa).transpose(1, 0, 2),
      pad_row(a_log), pad_row(dt_bias), norm_a.astype(F32).reshape(1, LANES))


def _ret_kernel(q_ref, k_ref, v_ref, g_ref, cos_ref, sin_ref, nb_ref, o_ref, r_ref, dm_ref, xi_ref, zeta_ref,
                *, n_heads):
    C = CHUNK
    tb = q_ref.shape[0]
    log_gammas = [math.log1p(-(2.0 ** (-5.0 - h))) for h in range(n_heads)]

    @pl.when(pl.program_id(1) == 0)
    def _():
        r_ref[...] = jnp.zeros_like(r_ref)
        row = lax.broadcasted_iota(jnp.int32, (C, C), 0)
        col = lax.broadcasted_iota(jnp.int32, (C, C), 1)
        rel = (row - col).astype(F32)
        idx = lax.broadcasted_iota(jnp.int32, (C, LANES), 0).astype(F32)
        for h in range(n_heads):
            dm_ref[h] = jnp.where(rel >= 0, jnp.exp(jnp.maximum(rel, 0.0) * log_gammas[h]), 0.0)
            xi_ref[h] = jnp.exp((idx + 1.0) * log_gammas[h])
            zeta_ref[h] = jnp.exp((C - 1.0 - idx) * log_gammas[h])

    scale = float(LANES) ** -0.5

    def chunk(c, carry):
        r0 = pl.multiple_of(c * C, C)
        cos = cos_ref[pl.ds(r0, C), :]
        sin = sin_ref[pl.ds(r0, C), :]
        for h in range(n_heads):
            lo = h * LANES
            sl = slice(lo, lo + LANES)
            g_chunk = math.exp(C * log_gammas[h])
            q = q_ref[pl.ds(r0, C), sl]
            k = k_ref[pl.ds(r0, C), sl]
            v = v_ref[pl.ds(r0, C), sl]
            q = q * cos + pltpu.roll(q, LANES // 2, 1) * sin
            k = (k * cos + pltpu.roll(k, LANES // 2, 1) * sin) * scale
            kb = k.astype(BF16)
            vb = v.astype(BF16)
            scores = _dot_nt(q.astype(BF16), kb) * dm_ref[h]
            r_h = r_ref[h]
            o = _dot(scores.astype(BF16), vb) + _dot((q * xi_ref[h]).astype(BF16), r_h.astype(BF16))
            r_ref[h] = g_chunk * r_h + _dot_tn((k * zeta_ref[h]).astype(BF16), vb)
            mu = jnp.mean(o, axis=-1, keepdims=True)
            oc = o - mu
            var = jnp.mean(oc * oc, axis=-1, keepdims=True)
            g = g_ref[pl.ds(r0, C), sl]
            o_ref[pl.ds(r0, C), sl] = (oc * lax.rsqrt(var + EPS) * nb_ref[:, sl] * _silu(g)).astype(o_ref.dtype)
        return carry

    lax.fori_loop(0, tb // C, chunk, 0)


def _retention(proj, norm_b, bsz, seq, wb, col0, tb=MIXER_TB):
    n_heads = wb // LANES
    nt = seq // tb
    half = LANES // 2
    pos = jnp.arange(seq, dtype=F32)
    theta = 1.0 / ROPE_BASE ** jnp.linspace(0.0, 1.0, half, dtype=F32)
    ang = pos[:, None] * theta[None, :]
    cos, sin = jnp.cos(ang), jnp.sin(ang)
    cos2 = jnp.concatenate([cos, cos], axis=-1)
    sin2 = jnp.concatenate([-sin, sin], axis=-1)

    def colblk(cb):
        return pl.BlockSpec((tb, wb), lambda b, t: (b * nt + t, col0 + cb))

    return pl.pallas_call(
        functools.partial(_ret_kernel, n_heads=n_heads),
        grid=(bsz, nt),
        in_specs=[colblk(0), colblk(1), colblk(2), colblk(3),
                  pl.BlockSpec((tb, LANES), lambda b, t: (t, 0)),
                  pl.BlockSpec((tb, LANES), lambda b, t: (t, 0)),
                  pl.BlockSpec((1, wb), lambda b, t: (0, 0))],
        out_specs=pl.BlockSpec((tb, wb), lambda b, t: (b * nt + t, 0)),
        out_shape=jax.ShapeDtypeStruct((bsz * seq, wb), BF16),
        scratch_shapes=[pltpu.VMEM((n_heads, LANES, LANES), F32), pltpu.VMEM((n_heads, CHUNK, CHUNK), F32),
                        pltpu.VMEM((n_heads, CHUNK, LANES), F32), pltpu.VMEM((n_heads, CHUNK, LANES), F32)],
        compiler_params=_params(("parallel", "arbitrary")),
        name="retention",
    )(proj, proj, proj, proj, cos2, sin2, norm_b.astype(F32).reshape(1, wb))


def _rglru_kernel(y_ref, x_ref, cw_ref, cb_ref, wa_ref, ba_ref, wx_ref, bx_ref, lam_ref,
                  o_ref, tail_ref, h_ref, *, n_groups):
    C = CHUNK
    tb = x_ref.shape[0]

    @pl.when(pl.program_id(1) == 0)
    def _():
        tail_ref[...] = jnp.zeros_like(tail_ref)
        h_ref[...] = jnp.zeros_like(h_ref)

    rows = lax.broadcasted_iota(jnp.int32, (C, LANES), 0)

    def chunk(c, carry):
        r0 = pl.multiple_of(c * C, C)
        for g in range(n_groups):
            lo = g * LANES
            sl = slice(lo, lo + LANES)
            u = _conv_silu(tail_ref, x_ref, cw_ref, r0, lo, bias=cb_ref[:, sl], act=False)
            ub = u.astype(BF16)
            r_gate = jax.nn.sigmoid(_dot(ub, wa_ref[g]) + ba_ref[:, sl])
            i_gate = jax.nn.sigmoid(_dot(ub, wx_ref[g]) + bx_ref[:, sl])
            log_a = -RG_C * r_gate * _softplus(-lam_ref[:, sl])
            a = jnp.exp(log_a)
            th = jnp.tanh(log_a)
            b = jnp.sqrt(-2.0 * th / (1.0 - th)) * (i_gate * u)
            d = 1
            while d < C:
                keep = rows >= d
                b = b + a * jnp.where(keep, pltpu.roll(b, d, 0), 0.0)
                a = a * jnp.where(keep, pltpu.roll(a, d, 0), 1.0)
                d *= 2
            h = b + a * h_ref[:, sl]
            h_ref[:, sl] = h[C - 1:C]
            y = y_ref[pl.ds(r0, C), sl]
            o_ref[pl.ds(r0, C), sl] = (jax.nn.gelu(y, approximate=True) * h).astype(o_ref.dtype)
        tail_ref[...] = x_ref[pl.ds(r0 + C - SUBLANES, SUBLANES), :]
        return carry

    lax.fori_loop(0, tb // C, chunk, 0)


def _rglru(proj, conv_w, conv_b, rg_wa, rg_ba, rg_wx, rg_bx, rg_lambda, bsz, seq, wc, tb=MIXER_TB):
    n_groups = wc // LANES
    nt = seq // tb
    row = lambda p: p.astype(F32).reshape(1, wc)
    small = lambda shape: pl.BlockSpec(shape, lambda b, t: (0,) * len(shape))
    return pl.pallas_call(
        functools.partial(_rglru_kernel, n_groups=n_groups),
        grid=(bsz, nt),
        in_specs=[pl.BlockSpec((tb, wc), lambda b, t: (b * nt + t, 0)),
                  pl.BlockSpec((tb, wc), lambda b, t: (b * nt + t, 1)),
                  small((CONV_K, wc)), small((1, wc)),
                  small((n_groups, LANES, LANES)), small((1, wc)),
                  small((n_groups, LANES, LANES)), small((1, wc)), small((1, wc))],
        out_specs=pl.BlockSpec((tb, wc), lambda b, t: (b * nt + t, 0)),
        out_shape=jax.ShapeDtypeStruct((bsz * seq, wc), BF16),
        scratch_shapes=[pltpu.VMEM((SUBLANES, wc), F32), pltpu.VMEM((1, wc), F32)],
        compiler_params=_params(("parallel", "arbitrary")),
        name="rglru",
    )(proj, proj, conv_w.astype(F32), row(conv_b), rg_wa.astype(BF16), row(rg_ba),
      rg_wx.astype(BF16), row(rg_bx), row(rg_lambda))


def _hgrn2_kernel(q_ref, f_ref, i_ref, g_ref, lb_ref, nd_ref, o_ref, st_ref, gam_ref, mask_ref, *, n_heads):
    C = CHUNK
    tb = q_ref.shape[0]
    levels = [C >> (l + 1) for l in range(int(math.log2(C)))]

    @pl.when(pl.program_id(1) == 0)
    def _():
        st_ref[...] = jnp.zeros_like(st_ref)
        row = lax.broadcasted_iota(jnp.int32, (C, C), 0)
        col = lax.broadcasted_iota(jnp.int32, (C, C), 1)
        mask_ref[0] = (row == col).astype(F32)
        for l, b in enumerate(levels):
            pair = ((row & -(2 * b)) == (col & -(2 * b))) & ((row & b) != 0) & ((col & b) == 0)
            mask_ref[l + 1] = pair.astype(F32)

    rows = lax.broadcasted_iota(jnp.int32, (C, LANES), 0)
    gain = nd_ref[...]

    def chunk(c, carry):
        r0 = pl.multiple_of(c * C, C)
        for h in range(n_heads):
            lo = h * LANES
            sl = slice(lo, lo + LANES)
            lb = lb_ref[:, sl]
            f_lin = f_ref[pl.ds(r0, C), sl]
            log_sig = jnp.minimum(f_lin, 0.0) - jnp.log(1.0 + jnp.exp(-jnp.abs(f_lin)))
            la, lc = jnp.log(lb), jnp.log1p(-lb) + log_sig
            log_f = jnp.maximum(la, lc) + jnp.log(1.0 + jnp.exp(-jnp.abs(la - lc)))
            k = (1.0 - lb) * jax.nn.sigmoid(-f_lin)
            q = _silu(q_ref[pl.ds(r0, C), sl])
            v = i_ref[pl.ds(r0, C), sl]
            vb = v.astype(BF16)
            gam = _cumsum_rows(log_f)
            gam_ref[...] = gam
            att = mask_ref[0] * _dot_nt(q.astype(BF16), k.astype(BF16))
            for l, b in enumerate(levels):
                nblk = C // (2 * b)
                if b >= SUBLANES // 2:
                    parts = [jnp.broadcast_to(gam_ref[pl.ds(j * 2 * b + b - 1, 1), :], (2 * b, LANES)) for j in range(nblk)]
                    gmid = jnp.concatenate(parts, axis=0) if nblk > 1 else parts[0]
                else:
                    off = (rows & (2 * b - 1)) - (b - 1)
                    gmid = jnp.zeros_like(gam)
                    for o_ in range(-(b - 1), b + 1):
                        gmid = jnp.where(off == o_, pltpu.roll(gam, o_ % C, 0), gmid)
                t = (jnp.where((rows & b) != 0, q, k) * jnp.exp(-jnp.abs(gam - gmid))).astype(BF16)
                att = att + mask_ref[l + 1] * _dot_nt(t, t)
            st = st_ref[h]
            o = _dot(att.astype(BF16), vb) + _dot_nt((q * jnp.exp(gam)).astype(BF16), st.astype(BF16))
            glast = gam[C - 1:C, :]
            kg = (k * jnp.exp(glast - gam)).astype(BF16)
            st_ref[h] = st * jnp.exp(glast) + _dot_tn(vb, kg)
            g = g_ref[pl.ds(r0, C), sl]
            o_ref[pl.ds(r0, C), sl] = _rms_gate(o, gain, g).astype(o_ref.dtype)
        return carry

    lax.fori_loop(0, tb // C, chunk, 0)


def _hgrn2(proj, lb, norm_d, bsz, seq, wd, col0, tb=MIXER_TB):
    n_heads = wd // LANES
    nt = seq // tb

    def colblk(cb):
        return pl.BlockSpec((tb, wd), lambda b, t: (b * nt + t, col0 + cb))

    return pl.pallas_call(
        functools.partial(_hgrn2_kernel, n_heads=n_heads),
        grid=(bsz, nt),
        in_specs=[colblk(0), colblk(1), colblk(2), colblk(3),
                  pl.BlockSpec((1, wd), lambda b, t: (0, 0)), pl.BlockSpec((1, LANES), lambda b, t: (0, 0))],
        out_specs=pl.BlockSpec((tb, wd), lambda b, t: (b * nt + t, 0)),
        out_shape=jax.ShapeDtypeStruct((bsz * seq, wd), BF16),
        scratch_shapes=[pltpu.VMEM((n_heads, LANES, LANES), F32), pltpu.VMEM((CHUNK, LANES), F32),
                        pltpu.VMEM((int(math.log2(CHUNK)) + 1, CHUNK, CHUNK), F32)],
        compiler_params=_params(("parallel", "arbitrary")),
        name="hgrn2",
    )(proj, proj, proj, proj, lb.astype(F32).reshape(1, wd), norm_d.astype(F32).reshape(1, LANES))


def _router_kernel(x_ref, g_ref, wr_ref, hn_ref, r_ref):
    x = x_ref[...]
    ms = jnp.mean(x * x, axis=-1, keepdims=True)
    hn = x * lax.rsqrt(ms + EPS) * g_ref[...]
    hn_ref[...] = hn
    w = wr_ref[...]
    h1 = hn.astype(BF16)
    h2 = (hn - h1.astype(F32)).astype(BF16)
    w1 = w.astype(BF16)
    w2 = (w - w1.astype(F32)).astype(BF16)
    logits = _dot(h1, w1) + (_dot(h1, w2) + _dot(h2, w1))
    lane = lax.broadcasted_iota(jnp.int32, logits.shape, 1)
    lg = jnp.where(lane < N_EXPERTS, logits, -jnp.inf)
    m1 = jnp.max(lg, axis=-1, keepdims=True)
    i1 = jnp.min(jnp.where(lg == m1, lane, LANES), axis=-1, keepdims=True)
    lg2 = jnp.where(lane == i1, -jnp.inf, lg)
    m2 = jnp.max(lg2, axis=-1, keepdims=True)
    i2 = jnp.min(jnp.where(lg2 == m2, lane, LANES), axis=-1, keepdims=True)
    e = jnp.exp(m2 - m1)
    g1 = 1.0 / (1.0 + e)
    g2 = e / (1.0 + e)
    r_ref[...] = jnp.where(lane == 0, i1.astype(F32), jnp.where(lane == 1, i2.astype(F32),
                           jnp.where(lane == 2, g1, jnp.where(lane == 3, g2, 0.0))))


def _router(x, g, w_router, tm=256):
    t, d = x.shape
    wr = jnp.zeros((d, LANES), F32).at[:, :N_EXPERTS].set(w_router.astype(F32))
    return pl.pallas_call(
        _router_kernel,
        grid=(t // tm,),
        in_specs=[pl.BlockSpec((tm, d), lambda i: (i, 0)), pl.BlockSpec((1, d), lambda i: (0, 0)),
                  pl.BlockSpec((d, LANES), lambda i: (0, 0))],
        out_specs=[pl.BlockSpec((tm, d), lambda i: (i, 0)), pl.BlockSpec((tm, LANES), lambda i: (i, 0))],
        out_shape=[jax.ShapeDtypeStruct((t, d), F32), jax.ShapeDtypeStruct((t, LANES), F32)],
        compiler_params=_params(("parallel",)),
        name="router",
    )(x, g.reshape(1, d), wr)


def _combine_kernel(dest_ref, dest_next_ref, r_ref, x_ref, y_ref, gf_ref, o_ref, buf_ref, sems):
    i = pl.program_id(0)
    tb = x_ref.shape[0]
    slot = lax.rem(i, 2)

    @pl.when(i == 0)
    def _():
        _gather_rows(dest_ref, TOP_K * tb, y_ref, buf_ref.at[0], sems.at[0])

    @pl.when(i + 1 < pl.num_programs(0))
    def _():
        _gather_rows(dest_next_ref, TOP_K * tb, y_ref, buf_ref.at[1 - slot], sems.at[1 - slot])

    _wait_rows(TOP_K * tb, y_ref, buf_ref.at[slot], sems.at[slot])
    gates = r_ref[...]
    acc = x_ref[...] + gates[:, 2:3] * buf_ref[slot, :tb] + gates[:, 3:4] * buf_ref[slot, tb:]
    ms = jnp.mean(acc * acc, axis=-1, keepdims=True)
    o_ref[...] = acc * lax.rsqrt(ms + EPS) * gf_ref[...]


def _combine(x, y_pad, dest, route, g_final, tb=256):
    t, d = x.shape
    nb = t // tb
    table = dest.reshape(nb, tb, TOP_K).transpose(0, 2, 1).reshape(nb, 1, TOP_K * tb)
    return pl.pallas_call(
        _combine_kernel,
        grid=(nb,),
        in_specs=[pl.BlockSpec((1, 1, TOP_K * tb), lambda i: (i, 0, 0), memory_space=pltpu.SMEM),
                  pl.BlockSpec((1, 1, TOP_K * tb), lambda i: (jnp.minimum(i + 1, nb - 1), 0, 0),
                               memory_space=pltpu.SMEM),
                  pl.BlockSpec((tb, LANES), lambda i: (i, 0)),
                  pl.BlockSpec((tb, d), lambda i: (i, 0)),
                  pl.BlockSpec(memory_space=pl.ANY),
                  pl.BlockSpec((1, d), lambda i: (0, 0))],
        out_specs=pl.BlockSpec((tb, d), lambda i: (i, 0)),
        out_shape=jax.ShapeDtypeStruct((t, d), F32),
        scratch_shapes=[pltpu.VMEM((2, TOP_K * tb, d), F32), pltpu.SemaphoreType.DMA((2,))],
        compiler_params=_params(("arbitrary",)),
        name="moe_combine",
    )(table, table, route, x, y_pad, g_final.reshape(1, d))


def _routing_tables(route, tm):
    t = route.shape[0]
    assert (t * TOP_K) % tm == 0
    flat_e = route[:, :TOP_K].astype(jnp.int32).reshape(-1)
    onehot = (flat_e[:, None] == jnp.arange(N_EXPERTS, dtype=jnp.int32)[None, :]).astype(jnp.int32)
    cum = jnp.cumsum(onehot, axis=0)
    rank = jnp.sum((cum - onehot) * onehot, axis=1)
    counts = cum[-1]
    padded = (counts + tm - 1) // tm * tm
    p_end = jnp.cumsum(padded)
    p_start = p_end - padded
    dest = (jnp.sum(onehot * p_start[None, :], axis=1) + rank).astype(jnp.int32)
    n_blocks = -(-(t * TOP_K) // tm) + N_EXPERTS
    block_e = jnp.minimum(jnp.searchsorted(p_end, jnp.arange(n_blocks, dtype=jnp.int32) * tm, side='right'),
                          N_EXPERTS - 1).astype(jnp.int32)
    n_valid = (p_end[-1] // tm).astype(jnp.int32).reshape(1)
    order = jnp.argsort(flat_e, stable=True).astype(jnp.int32)
    rows = jnp.arange(n_blocks * tm, dtype=jnp.int32)
    row_e = jnp.repeat(block_e, tm)
    row_rank = rows - p_start[row_e]
    entry = order[jnp.clip((jnp.cumsum(counts) - counts)[row_e] + row_rank, 0, t * TOP_K - 1)]
    src = jnp.where(row_rank < counts[row_e], entry // TOP_K, 0).astype(jnp.int32)
    return dest, src, block_e, n_valid


def kernel(x, norm_mix, norm_ffn, w_in_ab, conv_a, a_log, dt_bias, norm_a, norm_b, w_out_ab, w_gate_dense,
           w_up_dense, w_down_dense, w_in_cd, conv_c_w, conv_c_b, rg_wa, rg_ba, rg_wx, rg_bx, rg_lambda, hgrn_lb,
           norm_d, w_out_cd, w_router, w_gate_moe, w_up_moe, w_down_moe, norm_final):
    bsz, seq, d = x.shape
    xf = x.reshape(bsz * seq, d).astype(F32)
    x1 = _mixer_ab(xf, norm_mix[0], w_in_ab[0], conv_a[0], a_log[0], dt_bias[0], norm_a[0], norm_b[0], w_out_ab[0],
                   bsz, seq)
    x2 = _dense_ffn(x1, norm_ffn[0], w_gate_dense.astype(BF16), w_up_dense.astype(BF16), w_down_dense.astype(BF16))
    lb_soft = jax.nn.softmax(hgrn_lb.astype(F32), axis=0)
    lb_all = jnp.cumsum(lb_soft, axis=0) - lb_soft[0:1]
    x3 = _mixer_cd(x2, norm_mix[1], w_in_cd[0], conv_c_w[0], conv_c_b[0], rg_wa[0], rg_ba[0], rg_wx[0], rg_bx[0],
                   rg_lambda[0], lb_all[1], norm_d[0], w_out_cd[0], bsz, seq)
    out = _moe_ffn_final(x3, norm_ffn[1], w_router[0], w_gate_moe[0], w_up_moe[0], w_down_moe[0], norm_final)
    return out.reshape(bsz, seq, d).astype(x.dtype)


def _mixer_ab(xf, g_mix, w_in, conv_a, a_log, dt_bias, norm_a, norm_b, w_out, bsz, seq):
    wa = xf.shape[1] // 2
    n_ha = wa // LANES
    pad = lambda cols: jnp.pad(cols, ((0, 0), (0, LANES - cols.shape[1])))
    w_perm = jnp.concatenate([w_in[:, :4 * wa], w_in[:, 4 * wa + 2 * n_ha:],
                              pad(w_in[:, 4 * wa:4 * wa + n_ha]), pad(w_in[:, 4 * wa + n_ha:4 * wa + 2 * n_ha])],
                             axis=1).astype(BF16)
    proj = _rms_matmul(xf, g_mix, w_perm, name="in_proj_ab")
    o_a = _gdn(proj, conv_a, a_log, dt_bias, norm_a, bsz, seq, wa)
    o_b = _retention(proj, norm_b, bsz, seq, wa, col0=4)
    wo = w_out.astype(BF16)
    return _matmul([o_a, o_b], [wo[:wa], wo[wa:]], res=xf, tm=512, tn=2 * wa, name="out_proj_ab")


def _mixer_cd(x2, g_mix, w_in, conv_w, conv_b, rg_wa, rg_ba, rg_wx, rg_bx, rg_lambda, lb, norm_d, w_out, bsz, seq):
    wc = x2.shape[1] // 2
    proj = _rms_matmul(x2, g_mix, w_in.astype(BF16), name="in_proj_cd")
    o_c = _rglru(proj, conv_w, conv_b, rg_wa, rg_ba, rg_wx, rg_bx, rg_lambda, bsz, seq, wc)
    o_d = _hgrn2(proj, lb, norm_d, bsz, seq, wc, col0=2)
    wo = w_out.astype(BF16)
    return _matmul([o_c, o_d], [wo[:wc], wo[wc:]], res=x2, tm=512, tn=2 * wc, name="out_proj_cd")


def _moe_ffn_final(x3, g_ffn, w_router, w_gate, w_up, w_down, g_final, tm=1024):
    hn_moe, route = _router(x3, g_ffn, w_router)
    dest, src, block_e, n_valid = _routing_tables(route, tm)
    y_pad = _moe_ffn_call(hn_moe, src, w_gate.astype(BF16), w_up.astype(BF16), w_down.astype(BF16), block_e, n_valid,
                          tm=tm)
    return _combine(x3, y_pad, dest, route, g_final)
```

```python
import functools
import math

import jax
import jax.numpy as jnp
from jax import lax
from jax.experimental import pallas as pl
from jax.experimental.pallas import tpu as pltpu

F32 = jnp.float32
BF16 = jnp.bfloat16
LANES = 128
SUBLANES = 8
EPS = 1e-6
CONV_K = 4
RG_C = 8.0
ROPE_BASE = 10000.0
N_EXPERTS = 8
TOP_K = 2
CHUNK = 128
MIXER_TB = 512
VMEM_LIMIT = 48 * 2**20


def _params(sem):
    return pltpu.CompilerParams(dimension_semantics=sem, vmem_limit_bytes=VMEM_LIMIT)


def _dot(a, b):
    return jnp.dot(a, b, preferred_element_type=F32)


def _dot_nt(a, b):
    return lax.dot_general(a, b, (((1,), (1,)), ((), ())), preferred_element_type=F32)


def _dot_tn(a, b):
    return lax.dot_general(a, b, (((0,), (0,)), ((), ())), preferred_element_type=F32)


def _softplus(x):
    return jnp.maximum(x, 0.0) + jnp.log1p(jnp.exp(-jnp.abs(x)))


def _silu(x):
    return x * jax.nn.sigmoid(x)


def _cumsum_rows(x):
    n = x.shape[0]
    rows = lax.broadcasted_iota(jnp.int32, x.shape, 0)
    d = 1
    while d < n:
        x = x + jnp.where(rows >= d, pltpu.roll(x, d, 0), 0.0)
        d *= 2
    return x


def _rms(x, g):
    ms = jnp.mean(x * x, axis=-1, keepdims=True)
    return x * lax.rsqrt(ms + EPS) * g


def _rms_mm_kernel(x_ref, g_ref, w_ref, o_ref, hn_ref):
    @pl.when(pl.program_id(1) == 0)
    def _():
        hn_ref[...] = _rms(x_ref[...], g_ref[...]).astype(hn_ref.dtype)

    o_ref[...] = _dot(hn_ref[...], w_ref[...])


def _rms_matmul(x, g, w, tm=1024, tn=768, name="rms_matmul"):
    m, k = x.shape
    n = w.shape[1]
    tm = min(tm, m)
    return pl.pallas_call(
        _rms_mm_kernel,
        grid=(m // tm, n // tn),
        in_specs=[pl.BlockSpec((tm, k), lambda i, j: (i, 0)), pl.BlockSpec((1, k), lambda i, j: (0, 0)),
                  pl.BlockSpec((k, tn), lambda i, j: (0, j))],
        out_specs=pl.BlockSpec((tm, tn), lambda i, j: (i, j)),
        out_shape=jax.ShapeDtypeStruct((m, n), F32),
        scratch_shapes=[pltpu.VMEM((tm, k), BF16)],
        compiler_params=_params(("parallel", "arbitrary")),
        name=name,
    )(x, g.astype(F32).reshape(1, k), w)


def _mm_kernel(*refs, n_a, has_res):
    o_ref = refs[-1]
    acc = _dot(refs[0][...], refs[n_a][...])
    for t in range(1, n_a):
        acc = acc + _dot(refs[t][...], refs[n_a + t][...])
    if has_res:
        acc = acc + refs[2 * n_a][...]
    o_ref[...] = acc.astype(o_ref.dtype)


def _matmul(a_list, w_list, res=None, out_dtype=F32, tm=1024, tn=512, name="matmul"):
    m = a_list[0].shape[0]
    n = w_list[0].shape[1]
    tm, tn = min(tm, m), min(tn, n)
    in_specs = [pl.BlockSpec((tm, a.shape[1]), lambda i, j: (i, 0)) for a in a_list]
    in_specs += [pl.BlockSpec((w.shape[0], tn), lambda i, j: (0, j)) for w in w_list]
    args = list(a_list) + list(w_list)
    if res is not None:
        in_specs.append(pl.BlockSpec((tm, tn), lambda i, j: (i, j)))
        args.append(res)
    return pl.pallas_call(
        functools.partial(_mm_kernel, n_a=len(a_list), has_res=res is not None),
        grid=(m // tm, n // tn),
        in_specs=in_specs,
        out_specs=pl.BlockSpec((tm, tn), lambda i, j: (i, j)),
        out_shape=jax.ShapeDtypeStruct((m, n), out_dtype),
        compiler_params=_params(("parallel", "arbitrary")),
        name=name,
    )(*args)


def _swiglu_accumulate(x, wg_ref, wu_ref, wd_ref, o_ref):
    a = _silu(_dot(x, wg_ref[0])) * _dot(x, wu_ref[0])
    o_ref[...] += _dot(a.astype(BF16), wd_ref[0])


def _dense_ffn_kernel(x_ref, g_ref, wg_ref, wu_ref, wd_ref, o_ref, hn_ref):
    @pl.when(pl.program_id(1) == 0)
    def _():
        x = x_ref[...]
        o_ref[...] = x
        hn_ref[...] = _rms(x, g_ref[...]).astype(hn_ref.dtype)

    _swiglu_accumulate(hn_ref[...], wg_ref, wu_ref, wd_ref, o_ref)


def _dense_ffn(x, g, wg, wu, wd, tm=512, tf=512):
    m, d = x.shape
    f = wg.shape[2]
    return pl.pallas_call(
        _dense_ffn_kernel,
        grid=(m // tm, f // tf),
        in_specs=[pl.BlockSpec((tm, d), lambda i, j: (i, 0)),
                  pl.BlockSpec((1, d), lambda i, j: (0, 0)),
                  pl.BlockSpec((1, d, tf), lambda i, j: (0, 0, j)),
                  pl.BlockSpec((1, d, tf), lambda i, j: (0, 0, j)),
                  pl.BlockSpec((1, tf, d), lambda i, j: (0, j, 0))],
        out_specs=pl.BlockSpec((tm, d), lambda i, j: (i, 0)),
        out_shape=jax.ShapeDtypeStruct((m, d), F32),
        scratch_shapes=[pltpu.VMEM((tm, d), BF16)],
        compiler_params=_params(("parallel", "arbitrary")),
        name="ffn_dense",
    )(x, g.astype(F32).reshape(1, d), wg, wu, wd)


def _row_copy(src, s, dst, d, sem):
    return pltpu.make_async_copy(src.at[pl.ds(s, 1)], dst.at[pl.ds(d, 1)], sem)


DMA_UNROLL = 8


def _gather_rows(table_ref, n_rows, src, dst, sem):
    def body(g, carry):
        for u in range(DMA_UNROLL):
            r = g * DMA_UNROLL + u
            _row_copy(src, table_ref[0, 0, r], dst, r, sem).start(priority=u % 2)
        return carry
    lax.fori_loop(0, n_rows // DMA_UNROLL, body, 0)


def _wait_rows(n_rows, src, dst, sem):
    def body(r, carry):
        _row_copy(src, 0, dst, 0, sem).wait()
        return carry
    lax.fori_loop(0, n_rows, body, 0, unroll=DMA_UNROLL)


def _moe_ffn_kernel(be_ref, nv_ref, src_ref, src_next_ref, hn_ref, wg_ref, wu_ref, wd_ref, o_ref,
                    rows_ref, x_ref, sem, *, rows_per_step):
    i, j = pl.program_id(0), pl.program_id(1)
    n_valid = nv_ref[0]
    tm = x_ref.shape[0]
    n_rows = rows_ref.shape[0]

    @pl.when(j == 0)
    def _():
        o_ref[...] = jnp.zeros_like(o_ref)

    @pl.when(jnp.logical_and(i <= n_valid, j == 0))
    def _():
        @pl.when(i == 0)
        def _():
            _gather_rows(src_ref, n_rows, hn_ref, rows_ref, sem)

        _wait_rows(n_rows, hn_ref, rows_ref, sem)
        x_ref[...] = rows_ref[:tm].astype(BF16)

    @pl.when(i < n_valid)
    def _():
        for q in range(rows_per_step):
            r = j * rows_per_step + q
            _row_copy(hn_ref, src_next_ref[0, 0, r], rows_ref, r, sem).start(priority=q % 2)
        _swiglu_accumulate(x_ref[...], wg_ref, wu_ref, wd_ref, o_ref)


def _moe_ffn_call(hn, src, wg, wu, wd, block_e, n_valid, tm, tf=512):
    d = hn.shape[1]
    f = wg.shape[2]
    nb, nf = src.shape[0] // tm, f // tf
    rows_per_step = -(-tm // (nf * SUBLANES)) * SUBLANES
    n_rows = rows_per_step * nf
    src3 = jnp.pad(src.reshape(nb, 1, tm), ((0, 0), (0, 0), (0, n_rows - tm)))

    def wcol(i, j, be, nv):
        return (be[i], 0, jnp.where(i < nv[0], j, nf - 1))

    def wrow(i, j, be, nv):
        return (be[i], jnp.where(i < nv[0], j, nf - 1), 0)

    return pl.pallas_call(
        functools.partial(_moe_ffn_kernel, rows_per_step=rows_per_step),
        grid_spec=pltpu.PrefetchScalarGridSpec(
            num_scalar_prefetch=2,
            grid=(nb, nf),
            in_specs=[pl.BlockSpec((1, 1, n_rows), lambda i, j, be, nv: (i, 0, 0), memory_space=pltpu.SMEM),
                      pl.BlockSpec((1, 1, n_rows), lambda i, j, be, nv: (jnp.minimum(i + 1, nb - 1), 0, 0),
                                   memory_space=pltpu.SMEM),
                      pl.BlockSpec(memory_space=pl.ANY),
                      pl.BlockSpec((1, d, tf), wcol), pl.BlockSpec((1, d, tf), wcol), pl.BlockSpec((1, tf, d), wrow)],
            out_specs=pl.BlockSpec((tm, d), lambda i, j, be, nv: (i, 0)),
            scratch_shapes=[pltpu.VMEM((n_rows, d), hn.dtype), pltpu.VMEM((tm, d), BF16),
                            pltpu.SemaphoreType.DMA(())],
        ),
        out_shape=jax.ShapeDtypeStruct((nb * tm, d), F32),
        compiler_params=_params(("arbitrary", "arbitrary")),
        name="ffn_moe",
    )(block_e, n_valid, src3, src3, hn, wg, wu, wd)


def _conv_silu(tail_ref, x_ref, cw_ref, r0, lo, bias=None, act=True):
    sl = slice(lo, lo + LANES)
    win = jnp.concatenate([tail_ref[:, sl], x_ref[pl.ds(r0, CHUNK), sl]], axis=0)
    w = cw_ref[:, sl]
    y = win * w[CONV_K - 1:CONV_K]
    for k in range(1, CONV_K):
        y = y + pltpu.roll(win, k, 0) * w[CONV_K - 1 - k:CONV_K - k]
    y = y[SUBLANES:]
    if bias is not None:
        y = y + bias
    return _silu(y) if act else y


def _rms_gate(o, gain, z):
    ms = jnp.mean(o * o, axis=-1, keepdims=True)
    return o * lax.rsqrt(ms + EPS) * gain * _silu(z)


def _gdn_kernel(q_ref, k_ref, v_ref, z_ref, beta_ref, alpha_ref, cw_ref, alog_ref, dtb_ref, na_ref,
                o_ref, s_ref, tq_ref, tk_ref, tv_ref, *, n_heads):
    C = CHUNK
    tb = q_ref.shape[0]

    @pl.when(pl.program_id(1) == 0)
    def _():
        s_ref[...] = jnp.zeros_like(s_ref)
        tq_ref[...] = jnp.zeros_like(tq_ref)
        tk_ref[...] = jnp.zeros_like(tk_ref)
        tv_ref[...] = jnp.zeros_like(tv_ref)

    row = lax.broadcasted_iota(jnp.int32, (C, C), 0)
    col = lax.broadcasted_iota(jnp.int32, (C, C), 1)
    causal = row >= col
    strict = row > col
    eye = (row == col).astype(F32)
    neg_a = -jnp.exp(alog_ref[...])
    dtb = dtb_ref[...]
    gain = na_ref[...]
    scale = float(LANES) ** -0.5

    def chunk(c, carry):
        r0 = pl.multiple_of(c * C, C)
        beta_c = jax.nn.sigmoid(beta_ref[pl.ds(r0, C), :])
        g_c = neg_a * _softplus(alpha_ref[pl.ds(r0, C), :] + dtb)
        gam_c = _cumsum_rows(g_c)
        gam_t = gam_c.T
        heads = range(n_heads)
        low, att, rhs, wqg, kg, eglast = [], [], [], [], [], []
        for h in heads:
            lo = h * LANES
            q = _conv_silu(tq_ref, q_ref, cw_ref.at[0], r0, lo)
            k = _conv_silu(tk_ref, k_ref, cw_ref.at[1], r0, lo)
            v = _conv_silu(tv_ref, v_ref, cw_ref.at[2], r0, lo)
            q = q * lax.rsqrt(jnp.sum(q * q, axis=-1, keepdims=True) + EPS) * scale
            k = k * lax.rsqrt(jnp.sum(k * k, axis=-1, keepdims=True) + EPS)
            gcol = gam_c[:, h:h + 1]
            grow = gam_t[h:h + 1, :]
            bcol = beta_c[:, h:h + 1]
            dec = jnp.where(causal, jnp.exp(jnp.where(causal, gcol - grow, 0.0)), 0.0)
            kb = k * bcol
            m1 = _dot_nt(jnp.concatenate([kb, q], axis=0).astype(BF16), k.astype(BF16))
            low.append(jnp.where(strict, m1[:C] * dec, 0.0))
            att.append((m1[C:] * dec).astype(BF16))
            egam = jnp.exp(gcol)
            glast = gam_c[C - 1:C, h:h + 1]
            rhs.append(jnp.concatenate([v * bcol, kb * egam], axis=1).astype(BF16))
            wqg.append((q * egam).astype(BF16))
            kg.append((k * jnp.exp(glast - gcol)).astype(BF16))
            eglast.append(jnp.exp(glast))
        tinv = [eye - low[h] for h in heads]
        lp = [low[h].astype(BF16) for h in heads]
        for it in range(int(math.log2(C)) - 1):
            lp = [_dot(lp[h], lp[h]).astype(BF16) for h in heads]
            tinv = [tinv[h] + _dot(tinv[h].astype(BF16), lp[h]) for h in heads]
        sol = [_dot(tinv[h].astype(BF16), rhs[h]) for h in heads]
        s_old = [s_ref[h] for h in heads]
        wq = [_dot(jnp.concatenate([sol[h][:, LANES:].astype(BF16), wqg[h]], axis=0), s_old[h].astype(BF16))
              for h in heads]
        vnb = [(sol[h][:, :LANES] - wq[h][:C]).astype(BF16) for h in heads]
        for h in heads:
            s_ref[h] = eglast[h] * s_old[h] + _dot_tn(kg[h], vnb[h])
        for h in heads:
            lo = h * LANES
            o = wq[h][C:] + _dot(att[h], vnb[h])
            z = z_ref[pl.ds(r0, C), lo:lo + LANES]
            o_ref[pl.ds(r0, C), lo:lo + LANES] = _rms_gate(o, gain, z).astype(o_ref.dtype)
        tq_ref[...] = q_ref[pl.ds(r0 + C - SUBLANES, SUBLANES), :]
        tk_ref[...] = k_ref[pl.ds(r0 + C - SUBLANES, SUBLANES), :]
        tv_ref[...] = v_ref[pl.ds(r0 + C - SUBLANES, SUBLANES), :]
        return carry

    lax.fori_loop(0, tb // C, chunk, 0)


def _gdn(proj, conv_w, a_log, dt_bias, norm_a, bsz, seq, wa, tb=MIXER_TB):
    n_heads = wa // LANES
    nt = seq // tb
    gate_blk = 8 * wa // LANES

    def colblk(cb):
        return pl.BlockSpec((tb, wa), lambda b, t: (b * nt + t, cb))

    def pad_row(p):
        return jnp.zeros((1, LANES), F32).at[0, :n_heads].set(p.astype(F32))

    small = lambda shape: pl.BlockSpec(shape, lambda b, t: (0,) * len(shape))
    return pl.pallas_call(
        functools.partial(_gdn_kernel, n_heads=n_heads),
        grid=(bsz, nt),
        in_specs=[colblk(0), colblk(1), colblk(2), colblk(3),
                  pl.BlockSpec((tb, LANES), lambda b, t: (b * nt + t, gate_blk)),
                  pl.BlockSpec((tb, LANES), lambda b, t: (b * nt + t, gate_blk + 1)),
                  small((3, CONV_K, wa)), small((1, LANES)), small((1, LANES)), small((1, LANES))],
        out_specs=pl.BlockSpec((tb, wa), lambda b, t: (b * nt + t, 0)),
        out_shape=jax.ShapeDtypeStruct((bsz * seq, wa), BF16),
        scratch_shapes=[pltpu.VMEM((n_heads, LANES, LANES), F32),
                        pltpu.VMEM((SUBLANES, wa), F32), pltpu.VMEM((SUBLANES, wa), F32),
                        pltpu.VMEM((SUBLANES, wa), F32)],
        compiler_params=_params(("parallel", "arbitrary")),
        name="gdn",
    )(proj, proj, proj, proj, proj, proj,
      conv_w.astype(F32).reshape(CONV_K, 3, wa).transpose(1, 0, 2),
      pad_row(a_log), pad_row(dt_bias), norm_a.astype(F32).reshape(1, LANES))


def _ret_kernel(q_ref, k_ref, v_ref, g_ref, cos_ref, sin_ref, nb_ref, o_ref, r_ref, dm_ref, xi_ref, zeta_ref,
                *, n_heads):
    C = CHUNK
    tb = q_ref.shape[0]
    log_gammas = [math.log1p(-(2.0 ** (-5.0 - h))) for h in range(n_heads)]

    @pl.when(pl.program_id(1) == 0)
    def _():
        r_ref[...] = jnp.zeros_like(r_ref)
        row = lax.broadcasted_iota(jnp.int32, (C, C), 0)
        col = lax.broadcasted_iota(jnp.int32, (C, C), 1)
        rel = (row - col).astype(F32)
        idx = lax.broadcasted_iota(jnp.int32, (C, LANES), 0).astype(F32)
        for h in range(n_heads):
            dm_ref[h] = jnp.where(rel >= 0, jnp.exp(jnp.maximum(rel, 0.0) * log_gammas[h]), 0.0)
            xi_ref[h] = jnp.exp((idx + 1.0) * log_gammas[h])
            zeta_ref[h] = jnp.exp((C - 1.0 - idx) * log_gammas[h])

    scale = float(LANES) ** -0.5

    def chunk(c, carry):
        r0 = pl.multiple_of(c * C, C)
        cos = cos_ref[pl.ds(r0, C), :]
        sin = sin_ref[pl.ds(r0, C), :]
        heads = range(n_heads)
        qs, ks, vbs = [], [], []
        for h in heads:
            sl = slice(h * LANES, (h + 1) * LANES)
            q = q_ref[pl.ds(r0, C), sl]
            k = k_ref[pl.ds(r0, C), sl]
            qs.append(q * cos + pltpu.roll(q, LANES // 2, 1) * sin)
            ks.append((k * cos + pltpu.roll(k, LANES // 2, 1) * sin) * scale)
            vbs.append(v_ref[pl.ds(r0, C), sl].astype(BF16))
        scores = [(_dot_nt(qs[h].astype(BF16), ks[h].astype(BF16)) * dm_ref[h]).astype(BF16) for h in heads]
        r_old = [r_ref[h] for h in heads]
        outs = [_dot(scores[h], vbs[h]) + _dot((qs[h] * xi_ref[h]).astype(BF16), r_old[h].astype(BF16))
                for h in heads]
        for h in heads:
            r_ref[h] = math.exp(C * log_gammas[h]) * r_old[h] + _dot_tn((ks[h] * zeta_ref[h]).astype(BF16), vbs[h])
        for h in heads:
            sl = slice(h * LANES, (h + 1) * LANES)
            o = outs[h]
            mu = jnp.mean(o, axis=-1, keepdims=True)
            oc = o - mu
            var = jnp.mean(oc * oc, axis=-1, keepdims=True)
            g = g_ref[pl.ds(r0, C), sl]
            o_ref[pl.ds(r0, C), sl] = (oc * lax.rsqrt(var + EPS) * nb_ref[:, sl] * _silu(g)).astype(o_ref.dtype)
        return carry

    lax.fori_loop(0, tb // C, chunk, 0)


def _retention(proj, norm_b, bsz, seq, wb, col0, tb=MIXER_TB):
    n_heads = wb // LANES
    nt = seq // tb
    half = LANES // 2
    pos = jnp.arange(seq, dtype=F32)
    theta = 1.0 / ROPE_BASE ** jnp.linspace(0.0, 1.0, half, dtype=F32)
    ang = pos[:, None] * theta[None, :]
    cos, sin = jnp.cos(ang), jnp.sin(ang)
    cos2 = jnp.concatenate([cos, cos], axis=-1)
    sin2 = jnp.concatenate([-sin, sin], axis=-1)

    def colblk(cb):
        return pl.BlockSpec((tb, wb), lambda b, t: (b * nt + t, col0 + cb))

    return pl.pallas_call(
        functools.partial(_ret_kernel, n_heads=n_heads),
        grid=(bsz, nt),
        in_specs=[colblk(0), colblk(1), colblk(2), colblk(3),
                  pl.BlockSpec((tb, LANES), lambda b, t: (t, 0)),
                  pl.BlockSpec((tb, LANES), lambda b, t: (t, 0)),
                  pl.BlockSpec((1, wb), lambda b, t: (0, 0))],
        out_specs=pl.BlockSpec((tb, wb), lambda b, t: (b * nt + t, 0)),
        out_shape=jax.ShapeDtypeStruct((bsz * seq, wb), BF16),
        scratch_shapes=[pltpu.VMEM((n_heads, LANES, LANES), F32), pltpu.VMEM((n_heads, CHUNK, CHUNK), F32),
                        pltpu.VMEM((n_heads, CHUNK, LANES), F32), pltpu.VMEM((n_heads, CHUNK, LANES), F32)],
        compiler_params=_params(("parallel", "arbitrary")),
        name="retention",
    )(proj, proj, proj, proj, cos2, sin2, norm_b.astype(F32).reshape(1, wb))


def _rglru_kernel(y_ref, x_ref, cw_ref, cb_ref, wa_ref, ba_ref, wx_ref, bx_ref, lam_ref,
                  o_ref, tail_ref, h_ref, *, n_groups):
    C = CHUNK
    tb = x_ref.shape[0]

    @pl.when(pl.program_id(1) == 0)
    def _():
        tail_ref[...] = jnp.zeros_like(tail_ref)
        h_ref[...] = jnp.zeros_like(h_ref)

    rows = lax.broadcasted_iota(jnp.int32, (C, LANES), 0)

    def chunk(c, carry):
        r0 = pl.multiple_of(c * C, C)
        for g in range(n_groups):
            lo = g * LANES
            sl = slice(lo, lo + LANES)
            u = _conv_silu(tail_ref, x_ref, cw_ref, r0, lo, bias=cb_ref[:, sl], act=False)
            ub = u.astype(BF16)
            r_gate = jax.nn.sigmoid(_dot(ub, wa_ref[g]) + ba_ref[:, sl])
            i_gate = jax.nn.sigmoid(_dot(ub, wx_ref[g]) + bx_ref[:, sl])
            log_a = -RG_C * r_gate * _softplus(-lam_ref[:, sl])
            a = jnp.exp(log_a)
            th = jnp.tanh(log_a)
            b = jnp.sqrt(-2.0 * th / (1.0 - th)) * (i_gate * u)
            d = 1
            while d < C:
                keep = rows >= d
                b = b + a * jnp.where(keep, pltpu.roll(b, d, 0), 0.0)
                a = a * jnp.where(keep, pltpu.roll(a, d, 0), 1.0)
                d *= 2
            h = b + a * h_ref[:, sl]
            h_ref[:, sl] = h[C - 1:C]
            y = y_ref[pl.ds(r0, C), sl]
            o_ref[pl.ds(r0, C), sl] = (jax.nn.gelu(y, approximate=True) * h).astype(o_ref.dtype)
        tail_ref[...] = x_ref[pl.ds(r0 + C - SUBLANES, SUBLANES), :]
        return carry

    lax.fori_loop(0, tb // C, chunk, 0)


def _rglru(proj, conv_w, conv_b, rg_wa, rg_ba, rg_wx, rg_bx, rg_lambda, bsz, seq, wc, tb=MIXER_TB):
    n_groups = wc // LANES
    nt = seq // tb
    row = lambda p: p.astype(F32).reshape(1, wc)
    small = lambda shape: pl.BlockSpec(shape, lambda b, t: (0,) * len(shape))
    return pl.pallas_call(
        functools.partial(_rglru_kernel, n_groups=n_groups),
        grid=(bsz, nt),
        in_specs=[pl.BlockSpec((tb, wc), lambda b, t: (b * nt + t, 0)),
                  pl.BlockSpec((tb, wc), lambda b, t: (b * nt + t, 1)),
                  small((CONV_K, wc)), small((1, wc)),
                  small((n_groups, LANES, LANES)), small((1, wc)),
                  small((n_groups, LANES, LANES)), small((1, wc)), small((1, wc))],
        out_specs=pl.BlockSpec((tb, wc), lambda b, t: (b * nt + t, 0)),
        out_shape=jax.ShapeDtypeStruct((bsz * seq, wc), BF16),
        scratch_shapes=[pltpu.VMEM((SUBLANES, wc), F32), pltpu.VMEM((1, wc), F32)],
        compiler_params=_params(("parallel", "arbitrary")),
        name="rglru",
    )(proj, proj, conv_w.astype(F32), row(conv_b), rg_wa.astype(BF16), row(rg_ba),
      rg_wx.astype(BF16), row(rg_bx), row(rg_lambda))


def _hgrn2_kernel(q_ref, f_ref, i_ref, g_ref, lb_ref, nd_ref, o_ref, st_ref, gam_ref, mask_ref, *, n_heads):
    C = CHUNK
    tb = q_ref.shape[0]
    levels = [C >> (l + 1) for l in range(int(math.log2(C)))]

    @pl.when(pl.program_id(1) == 0)
    def _():
        st_ref[...] = jnp.zeros_like(st_ref)
        row = lax.broadcasted_iota(jnp.int32, (C, C), 0)
        col = lax.broadcasted_iota(jnp.int32, (C, C), 1)
        mask_ref[0] = (row == col).astype(F32)
        for l, b in enumerate(levels):
            pair = ((row & -(2 * b)) == (col & -(2 * b))) & ((row & b) != 0) & ((col & b) == 0)
            mask_ref[l + 1] = pair.astype(F32)

    rows = lax.broadcasted_iota(jnp.int32, (C, LANES), 0)
    gain = nd_ref[...]

    def chunk(c, carry):
        r0 = pl.multiple_of(c * C, C)
        for h in range(n_heads):
            lo = h * LANES
            sl = slice(lo, lo + LANES)
            lb = lb_ref[:, sl]
            f_lin = f_ref[pl.ds(r0, C), sl]
            log_sig = jnp.minimum(f_lin, 0.0) - jnp.log(1.0 + jnp.exp(-jnp.abs(f_lin)))
            la, lc = jnp.log(lb), jnp.log1p(-lb) + log_sig
            log_f = jnp.maximum(la, lc) + jnp.log(1.0 + jnp.exp(-jnp.abs(la - lc)))
            k = (1.0 - lb) * jax.nn.sigmoid(-f_lin)
            q = _silu(q_ref[pl.ds(r0, C), sl])
            v = i_ref[pl.ds(r0, C), sl]
            vb = v.astype(BF16)
            gam = _cumsum_rows(log_f)
            gam_ref[...] = gam
            att = mask_ref[0] * _dot_nt(q.astype(BF16), k.astype(BF16))
            for l, b in enumerate(levels):
                nblk = C // (2 * b)
                if b >= SUBLANES // 2:
                    parts = [jnp.broadcast_to(gam_ref[pl.ds(j * 2 * b + b - 1, 1), :], (2 * b, LANES)) for j in range(nblk)]
                    gmid = jnp.concatenate(parts, axis=0) if nblk > 1 else parts[0]
                else:
                    off = (rows & (2 * b - 1)) - (b - 1)
                    gmid = jnp.zeros_like(gam)
                    for o_ in range(-(b - 1), b + 1):
                        gmid = jnp.where(off == o_, pltpu.roll(gam, o_ % C, 0), gmid)
                t = (jnp.where((rows & b) != 0, q, k) * jnp.exp(-jnp.abs(gam - gmid))).astype(BF16)
                att = att + mask_ref[l + 1] * _dot_nt(t, t)
            st = st_ref[h]
            o = _dot(att.astype(BF16), vb) + _dot_nt((q * jnp.exp(gam)).astype(BF16), st.astype(BF16))
            glast = gam[C - 1:C, :]
            kg = (k * jnp.exp(glast - gam)).astype(BF16)
            st_ref[h] = st * jnp.exp(glast) + _dot_tn(vb, kg)
            g = g_ref[pl.ds(r0, C), sl]
            o_ref[pl.ds(r0, C), sl] = _rms_gate(o, gain, g).astype(o_ref.dtype)
        return carry

    lax.fori_loop(0, tb // C, chunk, 0)


def _hgrn2(proj, lb, norm_d, bsz, seq, wd, col0, tb=MIXER_TB):
    n_heads = wd // LANES
    nt = seq // tb

    def colblk(cb):
        return pl.BlockSpec((tb, wd), lambda b, t: (b * nt + t, col0 + cb))

    return pl.pallas_call(
        functools.partial(_hgrn2_kernel, n_heads=n_heads),
        grid=(bsz, nt),
        in_specs=[colblk(0), colblk(1), colblk(2), colblk(3),
                  pl.BlockSpec((1, wd), lambda b, t: (0, 0)), pl.BlockSpec((1, LANES), lambda b, t: (0, 0))],
        out_specs=pl.BlockSpec((tb, wd), lambda b, t: (b * nt + t, 0)),
        out_shape=jax.ShapeDtypeStruct((bsz * seq, wd), BF16),
        scratch_shapes=[pltpu.VMEM((n_heads, LANES, LANES), F32), pltpu.VMEM((CHUNK, LANES), F32),
                        pltpu.VMEM((int(math.log2(CHUNK)) + 1, CHUNK, CHUNK), F32)],
        compiler_params=_params(("parallel", "arbitrary")),
        name="hgrn2",
    )(proj, proj, proj, proj, lb.astype(F32).reshape(1, wd), norm_d.astype(F32).reshape(1, LANES))


def _router_kernel(x_ref, g_ref, wr_ref, hn_ref, r_ref):
    x = x_ref[...]
    ms = jnp.mean(x * x, axis=-1, keepdims=True)
    hn = x * lax.rsqrt(ms + EPS) * g_ref[...]
    hn_ref[...] = hn
    w = wr_ref[...]
    h1 = hn.astype(BF16)
    h2 = (hn - h1.astype(F32)).astype(BF16)
    w1 = w.astype(BF16)
    w2 = (w - w1.astype(F32)).astype(BF16)
    logits = _dot(h1, w1) + (_dot(h1, w2) + _dot(h2, w1))
    lane = lax.broadcasted_iota(jnp.int32, logits.shape, 1)
    lg = jnp.where(lane < N_EXPERTS, logits, -jnp.inf)
    m1 = jnp.max(lg, axis=-1, keepdims=True)
    i1 = jnp.min(jnp.where(lg == m1, lane, LANES), axis=-1, keepdims=True)
    lg2 = jnp.where(lane == i1, -jnp.inf, lg)
    m2 = jnp.max(lg2, axis=-1, keepdims=True)
    i2 = jnp.min(jnp.where(lg2 == m2, lane, LANES), axis=-1, keepdims=True)
    e = jnp.exp(m2 - m1)
    g1 = 1.0 / (1.0 + e)
    g2 = e / (1.0 + e)
    r_ref[...] = jnp.where(lane == 0, i1.astype(F32), jnp.where(lane == 1, i2.astype(F32),
                           jnp.where(lane == 2, g1, jnp.where(lane == 3, g2, 0.0))))


def _router(x, g, w_router, tm=256):
    t, d = x.shape
    wr = jnp.zeros((d, LANES), F32).at[:, :N_EXPERTS].set(w_router.astype(F32))
    return pl.pallas_call(
        _router_kernel,
        grid=(t // tm,),
        in_specs=[pl.BlockSpec((tm, d), lambda i: (i, 0)), pl.BlockSpec((1, d), lambda i: (0, 0)),
                  pl.BlockSpec((d, LANES), lambda i: (0, 0))],
        out_specs=[pl.BlockSpec((tm, d), lambda i: (i, 0)), pl.BlockSpec((tm, LANES), lambda i: (i, 0))],
        out_shape=[jax.ShapeDtypeStruct((t, d), F32), jax.ShapeDtypeStruct((t, LANES), F32)],
        compiler_params=_params(("parallel",)),
        name="router",
    )(x, g.reshape(1, d), wr)


def _combine_kernel(dest_ref, dest_next_ref, r_ref, x_ref, y_ref, gf_ref, o_ref, buf_ref, sems):
    i = pl.program_id(0)
    tb = x_ref.shape[0]
    slot = lax.rem(i, 2)

    @pl.when(i == 0)
    def _():
        _gather_rows(dest_ref, TOP_K * tb, y_ref, buf_ref.at[0], sems.at[0])

    @pl.when(i + 1 < pl.num_programs(0))
    def _():
        _gather_rows(dest_next_ref, TOP_K * tb, y_ref, buf_ref.at[1 - slot], sems.at[1 - slot])

    _wait_rows(TOP_K * tb, y_ref, buf_ref.at[slot], sems.at[slot])
    gates = r_ref[...]
    acc = x_ref[...] + gates[:, 2:3] * buf_ref[slot, :tb] + gates[:, 3:4] * buf_ref[slot, tb:]
    ms = jnp.mean(acc * acc, axis=-1, keepdims=True)
    o_ref[...] = acc * lax.rsqrt(ms + EPS) * gf_ref[...]


def _combine(x, y_pad, dest, route, g_final, tb=256):
    t, d = x.shape
    nb = t // tb
    table = dest.reshape(nb, tb, TOP_K).transpose(0, 2, 1).reshape(nb, 1, TOP_K * tb)
    return pl.pallas_call(
        _combine_kernel,
        grid=(nb,),
        in_specs=[pl.BlockSpec((1, 1, TOP_K * tb), lambda i: (i, 0, 0), memory_space=pltpu.SMEM),
                  pl.BlockSpec((1, 1, TOP_K * tb), lambda i: (jnp.minimum(i + 1, nb - 1), 0, 0),
                               memory_space=pltpu.SMEM),
                  pl.BlockSpec((tb, LANES), lambda i: (i, 0)),
                  pl.BlockSpec((tb, d), lambda i: (i, 0)),
                  pl.BlockSpec(memory_space=pl.ANY),
                  pl.BlockSpec((1, d), lambda i: (0, 0))],
        out_specs=pl.BlockSpec((tb, d), lambda i: (i, 0)),
        out_shape=jax.ShapeDtypeStruct((t, d), F32),
        scratch_shapes=[pltpu.VMEM((2, TOP_K * tb, d), F32), pltpu.SemaphoreType.DMA((2,))],
        compiler_params=_params(("arbitrary",)),
        name="moe_combine",
    )(table, table, route, x, y_pad, g_final.reshape(1, d))


def _routing_tables(route, tm):
    t = route.shape[0]
    assert (t * TOP_K) % tm == 0
    flat_e = route[:, :TOP_K].astype(jnp.int32).reshape(-1)
    onehot = (flat_e[:, None] == jnp.arange(N_EXPERTS, dtype=jnp.int32)[None, :]).astype(jnp.int32)
    cum = jnp.cumsum(onehot, axis=0)
    rank = jnp.sum((cum - onehot) * onehot, axis=1)
    counts = cum[-1]
    padded = (counts + tm - 1) // tm * tm
    p_end = jnp.cumsum(padded)
    p_start = p_end - padded
    dest = (jnp.sum(onehot * p_start[None, :], axis=1) + rank).astype(jnp.int32)
    n_blocks = -(-(t * TOP_K) // tm) + N_EXPERTS
    block_e = jnp.minimum(jnp.searchsorted(p_end, jnp.arange(n_blocks, dtype=jnp.int32) * tm, side='right'),
                          N_EXPERTS - 1).astype(jnp.int32)
    n_valid = (p_end[-1] // tm).astype(jnp.int32).reshape(1)
    order = jnp.argsort(flat_e, stable=True).astype(jnp.int32)
    rows = jnp.arange(n_blocks * tm, dtype=jnp.int32)
    row_e = jnp.repeat(block_e, tm)
    row_rank = rows - p_start[row_e]
    entry = order[jnp.clip((jnp.cumsum(counts) - counts)[row_e] + row_rank, 0, t * TOP_K - 1)]
    src = jnp.where(row_rank < counts[row_e], entry // TOP_K, 0).astype(jnp.int32)
    return dest, src, block_e, n_valid


def kernel(x, norm_mix, norm_ffn, w_in_ab, conv_a, a_log, dt_bias, norm_a, norm_b, w_out_ab, w_gate_dense,
           w_up_dense, w_down_dense, w_in_cd, conv_c_w, conv_c_b, rg_wa, rg_ba, rg_wx, rg_bx, rg_lambda, hgrn_lb,
           norm_d, w_out_cd, w_router, w_gate_moe, w_up_moe, w_down_moe, norm_final):
    bsz, seq, d = x.shape
    xf = x.reshape(bsz * seq, d).astype(F32)
    x1 = _mixer_ab(xf, norm_mix[0], w_in_ab[0], conv_a[0], a_log[0], dt_bias[0], norm_a[0], norm_b[0], w_out_ab[0],
                   bsz, seq)
    x2 = _dense_ffn(x1, norm_ffn[0], w_gate_dense.astype(BF16), w_up_dense.astype(BF16), w_down_dense.astype(BF16))
    lb_soft = jax.nn.softmax(hgrn_lb.astype(F32), axis=0)
    lb_all = jnp.cumsum(lb_soft, axis=0) - lb_soft[0:1]
    x3 = _mixer_cd(x2, norm_mix[1], w_in_cd[0], conv_c_w[0], conv_c_b[0], rg_wa[0], rg_ba[0], rg_wx[0], rg_bx[0],
                   rg_lambda[0], lb_all[1], norm_d[0], w_out_cd[0], bsz, seq)
    out = _moe_ffn_final(x3, norm_ffn[1], w_router[0], w_gate_moe[0], w_up_moe[0], w_down_moe[0], norm_final)
    return out.reshape(bsz, seq, d).astype(x.dtype)


def _mixer_ab(xf, g_mix, w_in, conv_a, a_log, dt_bias, norm_a, norm_b, w_out, bsz, seq):
    wa = xf.shape[1] // 2
    n_ha = wa // LANES
    pad = lambda cols: jnp.pad(cols, ((0, 0), (0, LANES - cols.shape[1])))
    w_perm = jnp.concatenate([w_in[:, :4 * wa], w_in[:, 4 * wa + 2 * n_ha:],
                              pad(w_in[:, 4 * wa:4 * wa + n_ha]), pad(w_in[:, 4 * wa + n_ha:4 * wa + 2 * n_ha])],
                             axis=1).astype(BF16)
    proj = _rms_matmul(xf, g_mix, w_perm, tn=768, name="in_proj_ab")
    o_a = _gdn(proj, conv_a, a_log, dt_bias, norm_a, bsz, seq, wa)
    o_b = _retention(proj, norm_b, bsz, seq, wa, col0=4)
    wo = w_out.astype(BF16)
    return _matmul([o_a, o_b], [wo[:wa], wo[wa:]], res=xf, tm=512, tn=2 * wa, name="out_proj_ab")


def _mixer_cd(x2, g_mix, w_in, conv_w, conv_b, rg_wa, rg_ba, rg_wx, rg_bx, rg_lambda, lb, norm_d, w_out, bsz, seq):
    wc = x2.shape[1] // 2
    proj = _rms_matmul(x2, g_mix, w_in.astype(BF16), tn=1536, name="in_proj_cd")
    o_c = _rglru(proj, conv_w, conv_b, rg_wa, rg_ba, rg_wx, rg_bx, rg_lambda, bsz, seq, wc)
    o_d = _hgrn2(proj, lb, norm_d, bsz, seq, wc, col0=2)
    wo = w_out.astype(BF16)
    return _matmul([o_c, o_d], [wo[:wc], wo[wc:]], res=x2, tm=512, tn=2 * wc, name="out_proj_cd")


def _moe_ffn_final(x3, g_ffn, w_router, w_gate, w_up, w_down, g_final, tm=1024):
    hn_moe, route = _router(x3, g_ffn, w_router)
    dest, src, block_e, n_valid = _routing_tables(route, tm)
    y_pad = _moe_ffn_call(hn_moe, src, w_gate.astype(BF16), w_up.astype(BF16), w_down.astype(BF16), block_e, n_valid,
                          tm=tm)
    return _combine(x3, y_pad, dest, route, g_final)
```

```python
import functools
import math

import jax
import jax.numpy as jnp
from jax import lax
from jax.experimental import pallas as pl
from jax.experimental.pallas import tpu as pltpu

F32 = jnp.float32
BF16 = jnp.bfloat16
LANES = 128
SUBLANES = 8
EPS = 1e-6
CONV_K = 4
RG_C = 8.0
ROPE_BASE = 10000.0
N_EXPERTS = 8
TOP_K = 2
CHUNK = 128
MIXER_TB = 512
VMEM_LIMIT = 48 * 2**20


def _params(sem):
    return pltpu.CompilerParams(dimension_semantics=sem, vmem_limit_bytes=VMEM_LIMIT)


def _dot(a, b):
    return jnp.dot(a, b, preferred_element_type=F32)


def _dot_nt(a, b):
    return lax.dot_general(a, b, (((1,), (1,)), ((), ())), preferred_element_type=F32)


def _dot_tn(a, b):
    return lax.dot_general(a, b, (((0,), (0,)), ((), ())), preferred_element_type=F32)


def _softplus(x):
    return jnp.maximum(x, 0.0) + jnp.log1p(jnp.exp(-jnp.abs(x)))


def _silu(x):
    return x * jax.nn.sigmoid(x)


def _cumsum_rows(x):
    n = x.shape[0]
    rows = lax.broadcasted_iota(jnp.int32, x.shape, 0)
    d = 1
    while d < n:
        x = x + jnp.where(rows >= d, pltpu.roll(x, d, 0), 0.0)
        d *= 2
    return x


def _rms(x, g):
    ms = jnp.mean(x * x, axis=-1, keepdims=True)
    return x * lax.rsqrt(ms + EPS) * g


def _rms_mm_kernel(x_ref, g_ref, w_ref, o_ref, hn_ref):
    @pl.when(pl.program_id(1) == 0)
    def _():
        hn_ref[...] = _rms(x_ref[...], g_ref[...]).astype(hn_ref.dtype)

    o_ref[...] = _dot(hn_ref[...], w_ref[...])


def _rms_matmul(x, g, w, tm=1024, tn=768, name="rms_matmul"):
    m, k = x.shape
    n = w.shape[1]
    tm = min(tm, m)
    return pl.pallas_call(
        _rms_mm_kernel,
        grid=(m // tm, n // tn),
        in_specs=[pl.BlockSpec((tm, k), lambda i, j: (i, 0)), pl.BlockSpec((1, k), lambda i, j: (0, 0)),
                  pl.BlockSpec((k, tn), lambda i, j: (0, j))],
        out_specs=pl.BlockSpec((tm, tn), lambda i, j: (i, j)),
        out_shape=jax.ShapeDtypeStruct((m, n), F32),
        scratch_shapes=[pltpu.VMEM((tm, k), BF16)],
        compiler_params=_params(("parallel", "arbitrary")),
        name=name,
    )(x, g.astype(F32).reshape(1, k), w)


def _mm_kernel(*refs, n_a, has_res):
    o_ref = refs[-1]
    acc = _dot(refs[0][...], refs[n_a][...])
    for t in range(1, n_a):
        acc = acc + _dot(refs[t][...], refs[n_a + t][...])
    if has_res:
        acc = acc + refs[2 * n_a][...]
    o_ref[...] = acc.astype(o_ref.dtype)


def _matmul(a_list, w_list, res=None, out_dtype=F32, tm=1024, tn=512, name="matmul"):
    m = a_list[0].shape[0]
    n = w_list[0].shape[1]
    tm, tn = min(tm, m), min(tn, n)
    in_specs = [pl.BlockSpec((tm, a.shape[1]), lambda i, j: (i, 0)) for a in a_list]
    in_specs += [pl.BlockSpec((w.shape[0], tn), lambda i, j: (0, j)) for w in w_list]
    args = list(a_list) + list(w_list)
    if res is not None:
        in_specs.append(pl.BlockSpec((tm, tn), lambda i, j: (i, j)))
        args.append(res)
    return pl.pallas_call(
        functools.partial(_mm_kernel, n_a=len(a_list), has_res=res is not None),
        grid=(m // tm, n // tn),
        in_specs=in_specs,
        out_specs=pl.BlockSpec((tm, tn), lambda i, j: (i, j)),
        out_shape=jax.ShapeDtypeStruct((m, n), out_dtype),
        compiler_params=_params(("parallel", "arbitrary")),
        name=name,
    )(*args)


FFN_TF = 512


def _swiglu_accumulate(x, wg, wu, wd, o_ref):
    a = _silu(_dot(x, wg)) * _dot(x, wu)
    o_ref[...] += _dot(a.astype(BF16), wd)


def _weight_tile_copies(w_hbm, e, j, bufs, sems, slot):
    tf = bufs[0].shape[2]
    cols = pl.ds(pl.multiple_of(j * tf, tf), tf)
    return (pltpu.make_async_copy(w_hbm[0].at[e, :, cols], bufs[0].at[slot], sems.at[0, slot]),
            pltpu.make_async_copy(w_hbm[1].at[e, :, cols], bufs[1].at[slot], sems.at[1, slot]),
            pltpu.make_async_copy(w_hbm[2].at[e, cols, :], bufs[2].at[slot], sems.at[2, slot]))


def _weight_scratch(d, tf):
    return [pltpu.VMEM((2, d, tf), BF16), pltpu.VMEM((2, d, tf), BF16), pltpu.VMEM((2, tf, d), BF16),
            pltpu.SemaphoreType.DMA((3, 2))]


def _dense_ffn_kernel(x_ref, g_ref, wg_hbm, wu_hbm, wd_hbm, o_ref, hn_ref, wg_buf, wu_buf, wd_buf, sems, *, nf):
    i = pl.program_id(0)
    n_steps = pl.num_programs(0) * nf
    w_hbm, bufs = (wg_hbm, wu_hbm, wd_hbm), (wg_buf, wu_buf, wd_buf)

    def tile(step):
        return _weight_tile_copies(w_hbm, 0, lax.rem(step, nf), bufs, sems, lax.rem(step, 2))

    @pl.when(i == 0)
    def _():
        for c in tile(0):
            c.start()

    x = x_ref[...]
    o_ref[...] = x
    hn_ref[...] = _rms(x, g_ref[...]).astype(hn_ref.dtype)

    def body(j, carry):
        step = i * nf + j
        for c in tile(step):
            c.wait()

        @pl.when(step + 1 < n_steps)
        def _():
            for c in tile(step + 1):
                c.start()

        slot = lax.rem(step, 2)
        _swiglu_accumulate(hn_ref[...], wg_buf[slot], wu_buf[slot], wd_buf[slot], o_ref)
        return carry

    lax.fori_loop(0, nf, body, 0)


def _dense_ffn(x, g, wg, wu, wd, tm=512, tf=FFN_TF):
    m, d = x.shape
    f = wg.shape[2]
    hbm = pl.BlockSpec(memory_space=pl.ANY)
    return pl.pallas_call(
        functools.partial(_dense_ffn_kernel, nf=f // tf),
        grid=(m // tm,),
        in_specs=[pl.BlockSpec((tm, d), lambda i: (i, 0)), pl.BlockSpec((1, d), lambda i: (0, 0)), hbm, hbm, hbm],
        out_specs=pl.BlockSpec((tm, d), lambda i: (i, 0)),
        out_shape=jax.ShapeDtypeStruct((m, d), F32),
        scratch_shapes=[pltpu.VMEM((tm, d), BF16)] + _weight_scratch(d, tf),
        compiler_params=_params(("arbitrary",)),
        name="ffn_dense",
    )(x, g.astype(F32).reshape(1, d), wg, wu, wd)


def _row_copy(src, s, dst, d, sem):
    return pltpu.make_async_copy(src.at[pl.ds(s, 1)], dst.at[pl.ds(d, 1)], sem)


DMA_UNROLL = 8


def _gather_rows(table_ref, n_rows, src, dst, sem):
    def body(r, carry):
        _row_copy(src, table_ref[0, 0, r], dst, r, sem).start()
        return carry
    lax.fori_loop(0, n_rows, body, 0, unroll=DMA_UNROLL)


def _wait_rows(n_rows, src, dst, sem):
    def body(r, carry):
        _row_copy(src, 0, dst, 0, sem).wait()
        return carry
    lax.fori_loop(0, n_rows, body, 0, unroll=DMA_UNROLL)


def _moe_ffn_kernel(be_ref, nv_ref, src_ref, src_next_ref, hn_ref, wg_hbm, wu_hbm, wd_hbm, o_ref,
                    rows_ref, x_ref, wg_buf, wu_buf, wd_buf, row_sem, w_sems, *, nf, rows_per_step):
    i = pl.program_id(0)
    n_valid = nv_ref[0]
    tm = x_ref.shape[0]
    n_rows = rows_ref.shape[0]
    w_hbm, bufs = (wg_hbm, wu_hbm, wd_hbm), (wg_buf, wu_buf, wd_buf)

    def tile(blk, j):
        return _weight_tile_copies(w_hbm, be_ref[blk], j, bufs, w_sems, lax.rem(blk * nf + j, 2))

    o_ref[...] = jnp.zeros_like(o_ref)

    @pl.when(jnp.logical_and(i == 0, n_valid > 0))
    def _():
        for c in tile(0, 0):
            c.start()

    @pl.when(i <= n_valid)
    def _():
        @pl.when(i == 0)
        def _():
            _gather_rows(src_ref, n_rows, hn_ref, rows_ref, row_sem)

        _wait_rows(n_rows, hn_ref, rows_ref, row_sem)
        x_ref[...] = rows_ref[:tm].astype(BF16)

    @pl.when(i < n_valid)
    def _():
        def body(j, carry):
            for c in tile(i, j):
                c.wait()
            last = j + 1 == nf
            nxt_blk, nxt_j = jnp.where(last, i + 1, i), jnp.where(last, 0, j + 1)

            @pl.when(nxt_blk < n_valid)
            def _():
                for c in tile(nxt_blk, nxt_j):
                    c.start()

            for q in range(rows_per_step):
                r = j * rows_per_step + q
                _row_copy(hn_ref, src_next_ref[0, 0, r], rows_ref, r, row_sem).start()
            slot = lax.rem(i * nf + j, 2)
            _swiglu_accumulate(x_ref[...], wg_buf[slot], wu_buf[slot], wd_buf[slot], o_ref)
            return carry

        lax.fori_loop(0, nf, body, 0)


def _moe_ffn_call(hn, src, wg, wu, wd, block_e, n_valid, tm, tf=FFN_TF):
    d = hn.shape[1]
    f = wg.shape[2]
    nb, nf = src.shape[0] // tm, f // tf
    rows_per_step = -(-tm // (nf * SUBLANES)) * SUBLANES
    n_rows = rows_per_step * nf
    src3 = jnp.pad(src.reshape(nb, 1, tm), ((0, 0), (0, 0), (0, n_rows - tm)))

    hbm = pl.BlockSpec(memory_space=pl.ANY)
    return pl.pallas_call(
        functools.partial(_moe_ffn_kernel, nf=nf, rows_per_step=rows_per_step),
        grid_spec=pltpu.PrefetchScalarGridSpec(
            num_scalar_prefetch=2,
            grid=(nb,),
            in_specs=[pl.BlockSpec((1, 1, n_rows), lambda i, be, nv: (i, 0, 0), memory_space=pltpu.SMEM),
                      pl.BlockSpec((1, 1, n_rows), lambda i, be, nv: (jnp.minimum(i + 1, nb - 1), 0, 0),
                                   memory_space=pltpu.SMEM),
                      hbm, hbm, hbm, hbm],
            out_specs=pl.BlockSpec((tm, d), lambda i, be, nv: (i, 0)),
            scratch_shapes=[pltpu.VMEM((n_rows, d), hn.dtype), pltpu.VMEM((tm, d), BF16)]
            + _weight_scratch(d, tf)[:3] + [pltpu.SemaphoreType.DMA(()), pltpu.SemaphoreType.DMA((3, 2))],
        ),
        out_shape=jax.ShapeDtypeStruct((nb * tm, d), F32),
        compiler_params=_params(("arbitrary",)),
        name="ffn_moe",
    )(block_e, n_valid, src3, src3, hn, wg, wu, wd)


def _conv_silu(tail_ref, x_ref, cw_ref, r0, lo, bias=None, act=True):
    sl = slice(lo, lo + LANES)
    win = jnp.concatenate([tail_ref[:, sl], x_ref[pl.ds(r0, CHUNK), sl]], axis=0)
    w = cw_ref[:, sl]
    y = win * w[CONV_K - 1:CONV_K]
    for k in range(1, CONV_K):
        y = y + pltpu.roll(win, k, 0) * w[CONV_K - 1 - k:CONV_K - k]
    y = y[SUBLANES:]
    if bias is not None:
        y = y + bias
    return _silu(y) if act else y


def _rms_gate(o, gain, z):
    ms = jnp.mean(o * o, axis=-1, keepdims=True)
    return o * lax.rsqrt(ms + EPS) * gain * _silu(z)


def _gdn_kernel(q_ref, k_ref, v_ref, z_ref, beta_ref, alpha_ref, cw_ref, alog_ref, dtb_ref, na_ref,
                o_ref, s_ref, tq_ref, tk_ref, tv_ref, *, n_heads):
    C = CHUNK
    tb = q_ref.shape[0]

    @pl.when(pl.program_id(1) == 0)
    def _():
        s_ref[...] = jnp.zeros_like(s_ref)
        tq_ref[...] = jnp.zeros_like(tq_ref)
        tk_ref[...] = jnp.zeros_like(tk_ref)
        tv_ref[...] = jnp.zeros_like(tv_ref)

    row = lax.broadcasted_iota(jnp.int32, (C, C), 0)
    col = lax.broadcasted_iota(jnp.int32, (C, C), 1)
    causal = row >= col
    strict = row > col
    eye = (row == col).astype(F32)
    neg_a = -jnp.exp(alog_ref[...])
    dtb = dtb_ref[...]
    gain = na_ref[...]
    scale = float(LANES) ** -0.5

    def chunk(c, carry):
        r0 = pl.multiple_of(c * C, C)
        beta_c = jax.nn.sigmoid(beta_ref[pl.ds(r0, C), :])
        g_c = neg_a * _softplus(alpha_ref[pl.ds(r0, C), :] + dtb)
        gam_c = _cumsum_rows(g_c)
        gam_t = gam_c.T
        heads = range(n_heads)
        low, att, rhs, wqg, kg, eglast = [], [], [], [], [], []
        for h in heads:
            lo = h * LANES
            q = _conv_silu(tq_ref, q_ref, cw_ref.at[0], r0, lo)
            k = _conv_silu(tk_ref, k_ref, cw_ref.at[1], r0, lo)
            v = _conv_silu(tv_ref, v_ref, cw_ref.at[2], r0, lo)
            q = q * lax.rsqrt(jnp.sum(q * q, axis=-1, keepdims=True) + EPS) * scale
            k = k * lax.rsqrt(jnp.sum(k * k, axis=-1, keepdims=True) + EPS)
            gcol = gam_c[:, h:h + 1]
            grow = gam_t[h:h + 1, :]
            bcol = beta_c[:, h:h + 1]
            dec = jnp.where(causal, jnp.exp(jnp.where(causal, gcol - grow, 0.0)), 0.0)
            kb = k * bcol
            m1 = _dot_nt(jnp.concatenate([kb, q], axis=0).astype(BF16), k.astype(BF16))
            low.append(jnp.where(strict, m1[:C] * dec, 0.0))
            att.append((m1[C:] * dec).astype(BF16))
            egam = jnp.exp(gcol)
            glast = gam_c[C - 1:C, h:h + 1]
            rhs.append(jnp.concatenate([v * bcol, kb * egam], axis=1).astype(BF16))
            wqg.append((q * egam).astype(BF16))
            kg.append((k * jnp.exp(glast - gcol)).astype(BF16))
            eglast.append(jnp.exp(glast))
        tinv = [eye - low[h] for h in heads]
        lp = [low[h].astype(BF16) for h in heads]
        for it in range(int(math.log2(C)) - 1):
            lp = [_dot(lp[h], lp[h]).astype(BF16) for h in heads]
            tinv = [tinv[h] + _dot(tinv[h].astype(BF16), lp[h]) for h in heads]
        sol = [_dot(tinv[h].astype(BF16), rhs[h]) for h in heads]
        s_old = [s_ref[h] for h in heads]
        wq = [_dot(jnp.concatenate([sol[h][:, LANES:].astype(BF16), wqg[h]], axis=0), s_old[h].astype(BF16))
              for h in heads]
        vnb = [(sol[h][:, :LANES] - wq[h][:C]).astype(BF16) for h in heads]
        for h in heads:
            s_ref[h] = eglast[h] * s_old[h] + _dot_tn(kg[h], vnb[h])
        for h in heads:
            lo = h * LANES
            o = wq[h][C:] + _dot(att[h], vnb[h])
            z = z_ref[pl.ds(r0, C), lo:lo + LANES]
            o_ref[pl.ds(r0, C), lo:lo + LANES] = _rms_gate(o, gain, z).astype(o_ref.dtype)
        tq_ref[...] = q_ref[pl.ds(r0 + C - SUBLANES, SUBLANES), :]
        tk_ref[...] = k_ref[pl.ds(r0 + C - SUBLANES, SUBLANES), :]
        tv_ref[...] = v_ref[pl.ds(r0 + C - SUBLANES, SUBLANES), :]
        return carry

    lax.fori_loop(0, tb // C, chunk, 0)


def _gdn(proj, conv_w, a_log, dt_bias, norm_a, bsz, seq, wa, tb=MIXER_TB):
    n_heads = wa // LANES
    nt = seq // tb
    gate_blk = 8 * wa // LANES

    def colblk(cb):
        return pl.BlockSpec((tb, wa), lambda b, t: (b * nt + t, cb))

    def pad_row(p):
        return jnp.zeros((1, LANES), F32).at[0, :n_heads].set(p.astype(F32))

    small = lambda shape: pl.BlockSpec(shape, lambda b, t: (0,) * len(shape))
    return pl.pallas_call(
        functools.partial(_gdn_kernel, n_heads=n_heads),
        grid=(bsz, nt),
        in_specs=[colblk(0), colblk(1), colblk(2), colblk(3),
                  pl.BlockSpec((tb, LANES), lambda b, t: (b * nt + t, gate_blk)),
                  pl.BlockSpec((tb, LANES), lambda b, t: (b * nt + t, gate_blk + 1)),
                  small((3, CONV_K, wa)), small((1, LANES)), small((1, LANES)), small((1, LANES))],
        out_specs=pl.BlockSpec((tb, wa), lambda b, t: (b * nt + t, 0)),
        out_shape=jax.ShapeDtypeStruct((bsz * seq, wa), BF16),
        scratch_shapes=[pltpu.VMEM((n_heads, LANES, LANES), F32),
                        pltpu.VMEM((SUBLANES, wa), F32), pltpu.VMEM((SUBLANES, wa), F32),
                        pltpu.VMEM((SUBLANES, wa), F32)],
        compiler_params=_params(("parallel", "arbitrary")),
        name="gdn",
    )(proj, proj, proj, proj, proj, proj,
      conv_w.astype(F32).reshape(CONV_K, 3, wa).transpose(1, 0, 2),
      pad_row(a_log), pad_row(dt_bias), norm_a.astype(F32).reshape(1, LANES))


def _ret_kernel(q_ref, k_ref, v_ref, g_ref, cos_ref, sin_ref, nb_ref, o_ref, r_ref, dm_ref, xi_ref, zeta_ref,
                *, n_heads):
    C = CHUNK
    tb = q_ref.shape[0]
    log_gammas = [math.log1p(-(2.0 ** (-5.0 - h))) for h in range(n_heads)]

    @pl.when(pl.program_id(1) == 0)
    def _():
        r_ref[...] = jnp.zeros_like(r_ref)
        row = lax.broadcasted_iota(jnp.int32, (C, C), 0)
        col = lax.broadcasted_iota(jnp.int32, (C, C), 1)
        rel = (row - col).astype(F32)
        idx = lax.broadcasted_iota(jnp.int32, (C, LANES), 0).astype(F32)
        for h in range(n_heads):
            dm_ref[h] = jnp.where(rel >= 0, jnp.exp(jnp.maximum(rel, 0.0) * log_gammas[h]), 0.0)
            xi_ref[h] = jnp.exp((idx + 1.0) * log_gammas[h])
            zeta_ref[h] = jnp.exp((C - 1.0 - idx) * log_gammas[h])

    scale = float(LANES) ** -0.5

    def chunk(c, carry):
        r0 = pl.multiple_of(c * C, C)
        cos = cos_ref[pl.ds(r0, C), :]
        sin = sin_ref[pl.ds(r0, C), :]
        heads = range(n_heads)
        qs, ks, vbs = [], [], []
        for h in heads:
            sl = slice(h * LANES, (h + 1) * LANES)
            q = q_ref[pl.ds(r0, C), sl]
            k = k_ref[pl.ds(r0, C), sl]
            qs.append(q * cos + pltpu.roll(q, LANES // 2, 1) * sin)
            ks.append((k * cos + pltpu.roll(k, LANES // 2, 1) * sin) * scale)
            vbs.append(v_ref[pl.ds(r0, C), sl].astype(BF16))
        scores = [(_dot_nt(qs[h].astype(BF16), ks[h].astype(BF16)) * dm_ref[h]).astype(BF16) for h in heads]
        r_old = [r_ref[h] for h in heads]
        outs = [_dot(scores[h], vbs[h]) + _dot((qs[h] * xi_ref[h]).astype(BF16), r_old[h].astype(BF16))
                for h in heads]
        for h in heads:
            r_ref[h] = math.exp(C * log_gammas[h]) * r_old[h] + _dot_tn((ks[h] * zeta_ref[h]).astype(BF16), vbs[h])
        for h in heads:
            sl = slice(h * LANES, (h + 1) * LANES)
            o = outs[h]
            mu = jnp.mean(o, axis=-1, keepdims=True)
            oc = o - mu
            var = jnp.mean(oc * oc, axis=-1, keepdims=True)
            g = g_ref[pl.ds(r0, C), sl]
            o_ref[pl.ds(r0, C), sl] = (oc * lax.rsqrt(var + EPS) * nb_ref[:, sl] * _silu(g)).astype(o_ref.dtype)
        return carry

    lax.fori_loop(0, tb // C, chunk, 0)


def _retention(proj, norm_b, bsz, seq, wb, col0, tb=MIXER_TB):
    n_heads = wb // LANES
    nt = seq // tb
    half = LANES // 2
    pos = jnp.arange(seq, dtype=F32)
    theta = 1.0 / ROPE_BASE ** jnp.linspace(0.0, 1.0, half, dtype=F32)
    ang = pos[:, None] * theta[None, :]
    cos, sin = jnp.cos(ang), jnp.sin(ang)
    cos2 = jnp.concatenate([cos, cos], axis=-1)
    sin2 = jnp.concatenate([-sin, sin], axis=-1)

    def colblk(cb):
        return pl.BlockSpec((tb, wb), lambda b, t: (b * nt + t, col0 + cb))

    return pl.pallas_call(
        functools.partial(_ret_kernel, n_heads=n_heads),
        grid=(bsz, nt),
        in_specs=[colblk(0), colblk(1), colblk(2), colblk(3),
                  pl.BlockSpec((tb, LANES), lambda b, t: (t, 0)),
                  pl.BlockSpec((tb, LANES), lambda b, t: (t, 0)),
                  pl.BlockSpec((1, wb), lambda b, t: (0, 0))],
        out_specs=pl.BlockSpec((tb, wb), lambda b, t: (b * nt + t, 0)),
        out_shape=jax.ShapeDtypeStruct((bsz * seq, wb), BF16),
        scratch_shapes=[pltpu.VMEM((n_heads, LANES, LANES), F32), pltpu.VMEM((n_heads, CHUNK, CHUNK), F32),
                        pltpu.VMEM((n_heads, CHUNK, LANES), F32), pltpu.VMEM((n_heads, CHUNK, LANES), F32)],
        compiler_params=_params(("parallel", "arbitrary")),
        name="retention",
    )(proj, proj, proj, proj, cos2, sin2, norm_b.astype(F32).reshape(1, wb))


def _rglru_kernel(y_ref, x_ref, cw_ref, cb_ref, wa_ref, ba_ref, wx_ref, bx_ref, lam_ref,
                  o_ref, tail_ref, h_ref, *, n_groups):
    C = CHUNK
    tb = x_ref.shape[0]

    @pl.when(pl.program_id(1) == 0)
    def _():
        tail_ref[...] = jnp.zeros_like(tail_ref)
        h_ref[...] = jnp.zeros_like(h_ref)

    rows = lax.broadcasted_iota(jnp.int32, (C, LANES), 0)

    def chunk(c, carry):
        r0 = pl.multiple_of(c * C, C)
        for g in range(n_groups):
            lo = g * LANES
            sl = slice(lo, lo + LANES)
            u = _conv_silu(tail_ref, x_ref, cw_ref, r0, lo, bias=cb_ref[:, sl], act=False)
            ub = u.astype(BF16)
            r_gate = jax.nn.sigmoid(_dot(ub, wa_ref[g]) + ba_ref[:, sl])
            i_gate = jax.nn.sigmoid(_dot(ub, wx_ref[g]) + bx_ref[:, sl])
            log_a = -RG_C * r_gate * _softplus(-lam_ref[:, sl])
            a = jnp.exp(log_a)
            th = jnp.tanh(log_a)
            b = jnp.sqrt(-2.0 * th / (1.0 - th)) * (i_gate * u)
            d = 1
            while d < C:
                keep = rows >= d
                b = b + a * jnp.where(keep, pltpu.roll(b, d, 0), 0.0)
                a = a * jnp.where(keep, pltpu.roll(a, d, 0), 1.0)
                d *= 2
            h = b + a * h_ref[:, sl]
            h_ref[:, sl] = h[C - 1:C]
            y = y_ref[pl.ds(r0, C), sl]
            o_ref[pl.ds(r0, C), sl] = (jax.nn.gelu(y, approximate=True) * h).astype(o_ref.dtype)
        tail_ref[...] = x_ref[pl.ds(r0 + C - SUBLANES, SUBLANES), :]
        return carry

    lax.fori_loop(0, tb // C, chunk, 0)


def _rglru(proj, conv_w, conv_b, rg_wa, rg_ba, rg_wx, rg_bx, rg_lambda, bsz, seq, wc, tb=MIXER_TB):
    n_groups = wc // LANES
    nt = seq // tb
    row = lambda p: p.astype(F32).reshape(1, wc)
    small = lambda shape: pl.BlockSpec(shape, lambda b, t: (0,) * len(shape))
    return pl.pallas_call(
        functools.partial(_rglru_kernel, n_groups=n_groups),
        grid=(bsz, nt),
        in_specs=[pl.BlockSpec((tb, wc), lambda b, t: (b * nt + t, 0)),
                  pl.BlockSpec((tb, wc), lambda b, t: (b * nt + t, 1)),
                  small((CONV_K, wc)), small((1, wc)),
                  small((n_groups, LANES, LANES)), small((1, wc)),
                  small((n_groups, LANES, LANES)), small((1, wc)), small((1, wc))],
        out_specs=pl.BlockSpec((tb, wc), lambda b, t: (b * nt + t, 0)),
        out_shape=jax.ShapeDtypeStruct((bsz * seq, wc), BF16),
        scratch_shapes=[pltpu.VMEM((SUBLANES, wc), F32), pltpu.VMEM((1, wc), F32)],
        compiler_params=_params(("parallel", "arbitrary")),
        name="rglru",
    )(proj, proj, conv_w.astype(F32), row(conv_b), rg_wa.astype(BF16), row(rg_ba),
      rg_wx.astype(BF16), row(rg_bx), row(rg_lambda))


def _hgrn2_kernel(q_ref, f_ref, i_ref, g_ref, lb_ref, nd_ref, o_ref, st_ref, gam_ref, mask_ref, *, n_heads):
    C = CHUNK
    tb = q_ref.shape[0]
    levels = [C >> (l + 1) for l in range(int(math.log2(C)))]

    @pl.when(pl.program_id(1) == 0)
    def _():
        st_ref[...] = jnp.zeros_like(st_ref)
        row = lax.broadcasted_iota(jnp.int32, (C, C), 0)
        col = lax.broadcasted_iota(jnp.int32, (C, C), 1)
        mask_ref[0] = (row == col).astype(F32)
        for l, b in enumerate(levels):
            pair = ((row & -(2 * b)) == (col & -(2 * b))) & ((row & b) != 0) & ((col & b) == 0)
            mask_ref[l + 1] = pair.astype(F32)

    rows = lax.broadcasted_iota(jnp.int32, (C, LANES), 0)
    gain = nd_ref[...]

    def chunk(c, carry):
        r0 = pl.multiple_of(c * C, C)
        for h in range(n_heads):
            lo = h * LANES
            sl = slice(lo, lo + LANES)
            lb = lb_ref[:, sl]
            f_lin = f_ref[pl.ds(r0, C), sl]
            log_sig = jnp.minimum(f_lin, 0.0) - jnp.log(1.0 + jnp.exp(-jnp.abs(f_lin)))
            la, lc = jnp.log(lb), jnp.log1p(-lb) + log_sig
            log_f = jnp.maximum(la, lc) + jnp.log(1.0 + jnp.exp(-jnp.abs(la - lc)))
            k = (1.0 - lb) * jax.nn.sigmoid(-f_lin)
            q = _silu(q_ref[pl.ds(r0, C), sl])
            v = i_ref[pl.ds(r0, C), sl]
            vb = v.astype(BF16)
            gam = _cumsum_rows(log_f)
            gam_ref[...] = gam
            att = mask_ref[0] * _dot_nt(q.astype(BF16), k.astype(BF16))
            for l, b in enumerate(levels):
                nblk = C // (2 * b)
                if b >= SUBLANES // 2:
                    parts = [jnp.broadcast_to(gam_ref[pl.ds(j * 2 * b + b - 1, 1), :], (2 * b, LANES)) for j in range(nblk)]
                    gmid = jnp.concatenate(parts, axis=0) if nblk > 1 else parts[0]
                else:
                    off = (rows & (2 * b - 1)) - (b - 1)
                    gmid = jnp.zeros_like(gam)
                    for o_ in range(-(b - 1), b + 1):
                        gmid = jnp.where(off == o_, pltpu.roll(gam, o_ % C, 0), gmid)
                t = (jnp.where((rows & b) != 0, q, k) * jnp.exp(-jnp.abs(gam - gmid))).astype(BF16)
                att = att + mask_ref[l + 1] * _dot_nt(t, t)
            st = st_ref[h]
            o = _dot(att.astype(BF16), vb) + _dot_nt((q * jnp.exp(gam)).astype(BF16), st.astype(BF16))
            glast = gam[C - 1:C, :]
            kg = (k * jnp.exp(glast - gam)).astype(BF16)
            st_ref[h] = st * jnp.exp(glast) + _dot_tn(vb, kg)
            g = g_ref[pl.ds(r0, C), sl]
            o_ref[pl.ds(r0, C), sl] = _rms_gate(o, gain, g).astype(o_ref.dtype)
        return carry

    lax.fori_loop(0, tb // C, chunk, 0)


def _hgrn2(proj, lb, norm_d, bsz, seq, wd, col0, tb=MIXER_TB):
    n_heads = wd // LANES
    nt = seq // tb

    def colblk(cb):
        return pl.BlockSpec((tb, wd), lambda b, t: (b * nt + t, col0 + cb))

    return pl.pallas_call(
        functools.partial(_hgrn2_kernel, n_heads=n_heads),
        grid=(bsz, nt),
        in_specs=[colblk(0), colblk(1), colblk(2), colblk(3),
                  pl.BlockSpec((1, wd), lambda b, t: (0, 0)), pl.BlockSpec((1, LANES), lambda b, t: (0, 0))],
        out_specs=pl.BlockSpec((tb, wd), lambda b, t: (b * nt + t, 0)),
        out_shape=jax.ShapeDtypeStruct((bsz * seq, wd), BF16),
        scratch_shapes=[pltpu.VMEM((n_heads, LANES, LANES), F32), pltpu.VMEM((CHUNK, LANES), F32),
                        pltpu.VMEM((int(math.log2(CHUNK)) + 1, CHUNK, CHUNK), F32)],
        compiler_params=_params(("parallel", "arbitrary")),
        name="hgrn2",
    )(proj, proj, proj, proj, lb.astype(F32).reshape(1, wd), norm_d.astype(F32).reshape(1, LANES))


def _router_kernel(x_ref, g_ref, wr_ref, hn_ref, r_ref):
    x = x_ref[...]
    ms = jnp.mean(x * x, axis=-1, keepdims=True)
    hn = x * lax.rsqrt(ms + EPS) * g_ref[...]
    hn_ref[...] = hn
    w = wr_ref[...]
    h1 = hn.astype(BF16)
    h2 = (hn - h1.astype(F32)).astype(BF16)
    w1 = w.astype(BF16)
    w2 = (w - w1.astype(F32)).astype(BF16)
    logits = _dot(h1, w1) + (_dot(h1, w2) + _dot(h2, w1))
    lane = lax.broadcasted_iota(jnp.int32, logits.shape, 1)
    lg = jnp.where(lane < N_EXPERTS, logits, -jnp.inf)
    m1 = jnp.max(lg, axis=-1, keepdims=True)
    i1 = jnp.min(jnp.where(lg == m1, lane, LANES), axis=-1, keepdims=True)
    lg2 = jnp.where(lane == i1, -jnp.inf, lg)
    m2 = jnp.max(lg2, axis=-1, keepdims=True)
    i2 = jnp.min(jnp.where(lg2 == m2, lane, LANES), axis=-1, keepdims=True)
    e = jnp.exp(m2 - m1)
    g1 = 1.0 / (1.0 + e)
    g2 = e / (1.0 + e)
    r_ref[...] = jnp.where(lane == 0, i1.astype(F32), jnp.where(lane == 1, i2.astype(F32),
                           jnp.where(lane == 2, g1, jnp.where(lane == 3, g2, 0.0))))


def _router(x, g, w_router, tm=256):
    t, d = x.shape
    wr = jnp.zeros((d, LANES), F32).at[:, :N_EXPERTS].set(w_router.astype(F32))
    return pl.pallas_call(
        _router_kernel,
        grid=(t // tm,),
        in_specs=[pl.BlockSpec((tm, d), lambda i: (i, 0)), pl.BlockSpec((1, d), lambda i: (0, 0)),
                  pl.BlockSpec((d, LANES), lambda i: (0, 0))],
        out_specs=[pl.BlockSpec((tm, d), lambda i: (i, 0)), pl.BlockSpec((tm, LANES), lambda i: (i, 0))],
        out_shape=[jax.ShapeDtypeStruct((t, d), F32), jax.ShapeDtypeStruct((t, LANES), F32)],
        compiler_params=_params(("parallel",)),
        name="router",
    )(x, g.reshape(1, d), wr)


def _combine_kernel(dest_ref, dest_next_ref, r_ref, x_ref, y_ref, gf_ref, o_ref, buf_ref, sems):
    i = pl.program_id(0)
    tb = x_ref.shape[0]
    slot = lax.rem(i, 2)

    @pl.when(i == 0)
    def _():
        _gather_rows(dest_ref, TOP_K * tb, y_ref, buf_ref.at[0], sems.at[0])

    @pl.when(i + 1 < pl.num_programs(0))
    def _():
        _gather_rows(dest_next_ref, TOP_K * tb, y_ref, buf_ref.at[1 - slot], sems.at[1 - slot])

    _wait_rows(TOP_K * tb, y_ref, buf_ref.at[slot], sems.at[slot])
    gates = r_ref[...]
    acc = x_ref[...] + gates[:, 2:3] * buf_ref[slot, :tb] + gates[:, 3:4] * buf_ref[slot, tb:]
    ms = jnp.mean(acc * acc, axis=-1, keepdims=True)
    o_ref[...] = acc * lax.rsqrt(ms + EPS) * gf_ref[...]


def _combine(x, y_pad, dest, route, g_final, tb=256):
    t, d = x.shape
    nb = t // tb
    table = dest.reshape(nb, tb, TOP_K).transpose(0, 2, 1).reshape(nb, 1, TOP_K * tb)
    return pl.pallas_call(
        _combine_kernel,
        grid=(nb,),
        in_specs=[pl.BlockSpec((1, 1, TOP_K * tb), lambda i: (i, 0, 0), memory_space=pltpu.SMEM),
                  pl.BlockSpec((1, 1, TOP_K * tb), lambda i: (jnp.minimum(i + 1, nb - 1), 0, 0),
                               memory_space=pltpu.SMEM),
                  pl.BlockSpec((tb, LANES), lambda i: (i, 0)),
                  pl.BlockSpec((tb, d), lambda i: (i, 0)),
                  pl.BlockSpec(memory_space=pl.ANY),
                  pl.BlockSpec((1, d), lambda i: (0, 0))],
        out_specs=pl.BlockSpec((tb, d), lambda i: (i, 0)),
        out_shape=jax.ShapeDtypeStruct((t, d), F32),
        scratch_shapes=[pltpu.VMEM((2, TOP_K * tb, d), F32), pltpu.SemaphoreType.DMA((2,))],
        compiler_params=_params(("arbitrary",)),
        name="moe_combine",
    )(table, table, route, x, y_pad, g_final.reshape(1, d))


def _routing_tables(route, tm):
    t = route.shape[0]
    assert (t * TOP_K) % tm == 0
    flat_e = route[:, :TOP_K].astype(jnp.int32).reshape(-1)
    onehot = (flat_e[:, None] == jnp.arange(N_EXPERTS, dtype=jnp.int32)[None, :]).astype(jnp.int32)
    cum = jnp.cumsum(onehot, axis=0)
    rank = jnp.sum((cum - onehot) * onehot, axis=1)
    counts = cum[-1]
    padded = (counts + tm - 1) // tm * tm
    p_end = jnp.cumsum(padded)
    p_start = p_end - padded
    dest = (jnp.sum(onehot * p_start[None, :], axis=1) + rank).astype(jnp.int32)
    n_blocks = -(-(t * TOP_K) // tm) + N_EXPERTS
    block_e = jnp.minimum(jnp.searchsorted(p_end, jnp.arange(n_blocks, dtype=jnp.int32) * tm, side='right'),
                          N_EXPERTS - 1).astype(jnp.int32)
    n_valid = (p_end[-1] // tm).astype(jnp.int32).reshape(1)
    order = jnp.argsort(flat_e, stable=True).astype(jnp.int32)
    rows = jnp.arange(n_blocks * tm, dtype=jnp.int32)
    row_e = jnp.repeat(block_e, tm)
    row_rank = rows - p_start[row_e]
    entry = order[jnp.clip((jnp.cumsum(counts) - counts)[row_e] + row_rank, 0, t * TOP_K - 1)]
    src = jnp.where(row_rank < counts[row_e], entry // TOP_K, 0).astype(jnp.int32)
    return dest, src, block_e, n_valid


def kernel(x, norm_mix, norm_ffn, w_in_ab, conv_a, a_log, dt_bias, norm_a, norm_b, w_out_ab, w_gate_dense,
           w_up_dense, w_down_dense, w_in_cd, conv_c_w, conv_c_b, rg_wa, rg_ba, rg_wx, rg_bx, rg_lambda, hgrn_lb,
           norm_d, w_out_cd, w_router, w_gate_moe, w_up_moe, w_down_moe, norm_final):
    bsz, seq, d = x.shape
    xf = x.reshape(bsz * seq, d).astype(F32)
    x1 = _mixer_ab(xf, norm_mix[0], w_in_ab[0], conv_a[0], a_log[0], dt_bias[0], norm_a[0], norm_b[0], w_out_ab[0],
                   bsz, seq)
    x2 = _dense_ffn(x1, norm_ffn[0], w_gate_dense.astype(BF16), w_up_dense.astype(BF16), w_down_dense.astype(BF16))
    lb_soft = jax.nn.softmax(hgrn_lb.astype(F32), axis=0)
    lb_all = jnp.cumsum(lb_soft, axis=0) - lb_soft[0:1]
    x3 = _mixer_cd(x2, norm_mix[1], w_in_cd[0], conv_c_w[0], conv_c_b[0], rg_wa[0], rg_ba[0], rg_wx[0], rg_bx[0],
                   rg_lambda[0], lb_all[1], norm_d[0], w_out_cd[0], bsz, seq)
    out = _moe_ffn_final(x3, norm_ffn[1], w_router[0], w_gate_moe[0], w_up_moe[0], w_down_moe[0], norm_final)
    return out.reshape(bsz, seq, d).astype(x.dtype)


def _mixer_ab(xf, g_mix, w_in, conv_a, a_log, dt_bias, norm_a, norm_b, w_out, bsz, seq):
    wa = xf.shape[1] // 2
    n_ha = wa // LANES
    pad = lambda cols: jnp.pad(cols, ((0, 0), (0, LANES - cols.shape[1])))
    w_perm = jnp.concatenate([w_in[:, :4 * wa], w_in[:, 4 * wa + 2 * n_ha:],
                              pad(w_in[:, 4 * wa:4 * wa + n_ha]), pad(w_in[:, 4 * wa + n_ha:4 * wa + 2 * n_ha])],
                             axis=1).astype(BF16)
    proj = _rms_matmul(xf, g_mix, w_perm, tm=512, tn=2816, name="in_proj_ab")
    o_a = _gdn(proj, conv_a, a_log, dt_bias, norm_a, bsz, seq, wa)
    o_b = _retention(proj, norm_b, bsz, seq, wa, col0=4)
    wo = w_out.astype(BF16)
    return _matmul([o_a, o_b], [wo[:wa], wo[wa:]], res=xf, tm=512, tn=2 * wa, name="out_proj_ab")


def _mixer_cd(x2, g_mix, w_in, conv_w, conv_b, rg_wa, rg_ba, rg_wx, rg_bx, rg_lambda, lb, norm_d, w_out, bsz, seq):
    wc = x2.shape[1] // 2
    proj = _rms_matmul(x2, g_mix, w_in.astype(BF16), tn=1536, name="in_proj_cd")
    o_c = _rglru(proj, conv_w, conv_b, rg_wa, rg_ba, rg_wx, rg_bx, rg_lambda, bsz, seq, wc)
    o_d = _hgrn2(proj, lb, norm_d, bsz, seq, wc, col0=2)
    wo = w_out.astype(BF16)
    return _matmul([o_c, o_d], [wo[:wc], wo[wc:]], res=x2, tm=512, tn=2 * wc, name="out_proj_cd")


def _moe_ffn_final(x3, g_ffn, w_router, w_gate, w_up, w_down, g_final, tm=1024):
    hn_moe, route = _router(x3, g_ffn, w_router)
    dest, src, block_e, n_valid = _routing_tables(route, tm)
    y_pad = _moe_ffn_call(hn_moe, src, w_gate.astype(BF16), w_up.astype(BF16), w_down.astype(BF16), block_e, n_valid,
                          tm=tm)
    return _combine(x3, y_pad, dest, route, g_final)
```

```python
import functools
import math

import jax
import jax.numpy as jnp
from jax import lax
from jax.experimental import pallas as pl
from jax.experimental.pallas import tpu as pltpu

F32 = jnp.float32
BF16 = jnp.bfloat16
LANES = 128
SUBLANES = 8
EPS = 1e-6
CONV_K = 4
RG_C = 8.0
ROPE_BASE = 10000.0
N_EXPERTS = 8
TOP_K = 2
CHUNK = 128
MIXER_TB = 512
VMEM_LIMIT = 48 * 2**20


def _params(sem):
    return pltpu.CompilerParams(dimension_semantics=sem, vmem_limit_bytes=VMEM_LIMIT)


def _dot(a, b):
    return jnp.dot(a, b, preferred_element_type=F32)


def _dot_nt(a, b):
    return lax.dot_general(a, b, (((1,), (1,)), ((), ())), preferred_element_type=F32)


def _dot_tn(a, b):
    return lax.dot_general(a, b, (((0,), (0,)), ((), ())), preferred_element_type=F32)


def _softplus(x):
    return jnp.maximum(x, 0.0) + jnp.log1p(jnp.exp(-jnp.abs(x)))


def _silu(x):
    return x * jax.nn.sigmoid(x)


def _cumsum_rows(x):
    n = x.shape[0]
    rows = lax.broadcasted_iota(jnp.int32, x.shape, 0)
    d = 1
    while d < n:
        x = x + jnp.where(rows >= d, pltpu.roll(x, d, 0), 0.0)
        d *= 2
    return x


def _rms(x, g):
    ms = jnp.mean(x * x, axis=-1, keepdims=True)
    return x * lax.rsqrt(ms + EPS) * g


def _rms_mm_kernel(x_ref, g_ref, w_ref, o_ref, hn_ref):
    @pl.when(pl.program_id(1) == 0)
    def _():
        hn_ref[...] = _rms(x_ref[...], g_ref[...]).astype(hn_ref.dtype)

    o_ref[...] = _dot(hn_ref[...], w_ref[...])


def _rms_matmul(x, g, w, tm=1024, tn=768, name="rms_matmul"):
    m, k = x.shape
    n = w.shape[1]
    tm = min(tm, m)
    return pl.pallas_call(
        _rms_mm_kernel,
        grid=(m // tm, n // tn),
        in_specs=[pl.BlockSpec((tm, k), lambda i, j: (i, 0)), pl.BlockSpec((1, k), lambda i, j: (0, 0)),
                  pl.BlockSpec((k, tn), lambda i, j: (0, j))],
        out_specs=pl.BlockSpec((tm, tn), lambda i, j: (i, j)),
        out_shape=jax.ShapeDtypeStruct((m, n), F32),
        scratch_shapes=[pltpu.VMEM((tm, k), BF16)],
        compiler_params=_params(("parallel", "arbitrary")),
        name=name,
    )(x, g.astype(F32).reshape(1, k), w)


def _mm_kernel(*refs, n_a, has_res):
    o_ref = refs[-1]
    acc = _dot(refs[0][...], refs[n_a][...])
    for t in range(1, n_a):
        acc = acc + _dot(refs[t][...], refs[n_a + t][...])
    if has_res:
        acc = acc + refs[2 * n_a][...]
    o_ref[...] = acc.astype(o_ref.dtype)


def _matmul(a_list, w_list, res=None, out_dtype=F32, tm=1024, tn=512, name="matmul"):
    m = a_list[0].shape[0]
    n = w_list[0].shape[1]
    tm, tn = min(tm, m), min(tn, n)
    in_specs = [pl.BlockSpec((tm, a.shape[1]), lambda i, j: (i, 0)) for a in a_list]
    in_specs += [pl.BlockSpec((w.shape[0], tn), lambda i, j: (0, j)) for w in w_list]
    args = list(a_list) + list(w_list)
    if res is not None:
        in_specs.append(pl.BlockSpec((tm, tn), lambda i, j: (i, j)))
        args.append(res)
    return pl.pallas_call(
        functools.partial(_mm_kernel, n_a=len(a_list), has_res=res is not None),
        grid=(m // tm, n // tn),
        in_specs=in_specs,
        out_specs=pl.BlockSpec((tm, tn), lambda i, j: (i, j)),
        out_shape=jax.ShapeDtypeStruct((m, n), out_dtype),
        compiler_params=_params(("parallel", "arbitrary")),
        name=name,
    )(*args)


FFN_TF = 512


def _swiglu_accumulate(x, wg, wu, wd, o_ref):
    a = _silu(_dot(x, wg)) * _dot(x, wu)
    o_ref[...] += _dot(a.astype(BF16), wd)


def _weight_tile_copies(w_hbm, e, j, bufs, sems, slot):
    tf = bufs[0].shape[2]
    cols = pl.ds(pl.multiple_of(j * tf, tf), tf)
    return (pltpu.make_async_copy(w_hbm[0].at[e, :, cols], bufs[0].at[slot], sems.at[0, slot]),
            pltpu.make_async_copy(w_hbm[1].at[e, :, cols], bufs[1].at[slot], sems.at[1, slot]),
            pltpu.make_async_copy(w_hbm[2].at[e, cols, :], bufs[2].at[slot], sems.at[2, slot]))


def _weight_scratch(d, tf):
    return [pltpu.VMEM((2, d, tf), BF16), pltpu.VMEM((2, d, tf), BF16), pltpu.VMEM((2, tf, d), BF16),
            pltpu.SemaphoreType.DMA((3, 2))]


def _dense_ffn_kernel(x_ref, g_ref, wg_hbm, wu_hbm, wd_hbm, o_ref, hn_ref, wg_buf, wu_buf, wd_buf, sems, *, nf):
    i = pl.program_id(0)
    n_steps = pl.num_programs(0) * nf
    w_hbm, bufs = (wg_hbm, wu_hbm, wd_hbm), (wg_buf, wu_buf, wd_buf)

    def tile(step):
        return _weight_tile_copies(w_hbm, 0, lax.rem(step, nf), bufs, sems, lax.rem(step, 2))

    @pl.when(i == 0)
    def _():
        for c in tile(0):
            c.start()

    x = x_ref[...]
    o_ref[...] = x
    hn_ref[...] = _rms(x, g_ref[...]).astype(hn_ref.dtype)

    def body(j, carry):
        step = i * nf + j
        for c in tile(step):
            c.wait()

        @pl.when(step + 1 < n_steps)
        def _():
            for c in tile(step + 1):
                c.start()

        slot = lax.rem(step, 2)
        _swiglu_accumulate(hn_ref[...], wg_buf[slot], wu_buf[slot], wd_buf[slot], o_ref)
        return carry

    lax.fori_loop(0, nf, body, 0)


def _dense_ffn(x, g, wg, wu, wd, tm=512, tf=FFN_TF):
    m, d = x.shape
    f = wg.shape[2]
    hbm = pl.BlockSpec(memory_space=pl.ANY)
    return pl.pallas_call(
        functools.partial(_dense_ffn_kernel, nf=f // tf),
        grid=(m // tm,),
        in_specs=[pl.BlockSpec((tm, d), lambda i: (i, 0)), pl.BlockSpec((1, d), lambda i: (0, 0)), hbm, hbm, hbm],
        out_specs=pl.BlockSpec((tm, d), lambda i: (i, 0)),
        out_shape=jax.ShapeDtypeStruct((m, d), F32),
        scratch_shapes=[pltpu.VMEM((tm, d), BF16)] + _weight_scratch(d, tf),
        compiler_params=_params(("arbitrary",)),
        name="ffn_dense",
    )(x, g.astype(F32).reshape(1, d), wg, wu, wd)


def _row_copy(src, s, dst, d, sem):
    return pltpu.make_async_copy(src.at[pl.ds(s, 1)], dst.at[pl.ds(d, 1)], sem)


DMA_UNROLL = 8


def _gather_rows(table_ref, n_rows, src, dst, sem):
    def body(r, carry):
        _row_copy(src, table_ref[0, 0, r], dst, r, sem).start()
        return carry
    lax.fori_loop(0, n_rows, body, 0, unroll=DMA_UNROLL)


def _wait_rows(n_rows, src, dst, sem):
    def body(r, carry):
        _row_copy(src, 0, dst, 0, sem).wait()
        return carry
    lax.fori_loop(0, n_rows, body, 0, unroll=DMA_UNROLL)


def _moe_ffn_kernel(be_ref, nv_ref, src_ref, src_next_ref, hn_ref, wg_hbm, wu_hbm, wd_hbm, o_ref,
                    rows_ref, x_ref, wg_buf, wu_buf, wd_buf, row_sem, w_sems, *, nf, rows_per_step):
    i = pl.program_id(0)
    n_valid = nv_ref[0]
    tm = x_ref.shape[0]
    n_rows = rows_ref.shape[0]
    w_hbm, bufs = (wg_hbm, wu_hbm, wd_hbm), (wg_buf, wu_buf, wd_buf)

    def tile(blk, j):
        return _weight_tile_copies(w_hbm, be_ref[blk], j, bufs, w_sems, lax.rem(blk * nf + j, 2))

    o_ref[...] = jnp.zeros_like(o_ref)

    @pl.when(jnp.logical_and(i == 0, n_valid > 0))
    def _():
        for c in tile(0, 0):
            c.start()

    @pl.when(i <= n_valid)
    def _():
        @pl.when(i == 0)
        def _():
            _gather_rows(src_ref, n_rows, hn_ref, rows_ref, row_sem)

        _wait_rows(n_rows, hn_ref, rows_ref, row_sem)
        x_ref[...] = rows_ref[:tm].astype(BF16)

    @pl.when(i < n_valid)
    def _():
        def body(j, carry):
            for c in tile(i, j):
                c.wait()
            last = j + 1 == nf
            nxt_blk, nxt_j = jnp.where(last, i + 1, i), jnp.where(last, 0, j + 1)

            @pl.when(nxt_blk < n_valid)
            def _():
                for c in tile(nxt_blk, nxt_j):
                    c.start()

            for q in range(rows_per_step):
                r = j * rows_per_step + q
                _row_copy(hn_ref, src_next_ref[0, 0, r], rows_ref, r, row_sem).start()
            slot = lax.rem(i * nf + j, 2)
            _swiglu_accumulate(x_ref[...], wg_buf[slot], wu_buf[slot], wd_buf[slot], o_ref)
            return carry

        lax.fori_loop(0, nf, body, 0)


def _moe_ffn_call(hn, src, wg, wu, wd, block_e, n_valid, tm, tf=FFN_TF):
    d = hn.shape[1]
    f = wg.shape[2]
    nb, nf = src.shape[0] // tm, f // tf
    rows_per_step = -(-tm // (nf * SUBLANES)) * SUBLANES
    n_rows = rows_per_step * nf
    src3 = jnp.pad(src.reshape(nb, 1, tm), ((0, 0), (0, 0), (0, n_rows - tm)))

    hbm = pl.BlockSpec(memory_space=pl.ANY)
    return pl.pallas_call(
        functools.partial(_moe_ffn_kernel, nf=nf, rows_per_step=rows_per_step),
        grid_spec=pltpu.PrefetchScalarGridSpec(
            num_scalar_prefetch=2,
            grid=(nb,),
            in_specs=[pl.BlockSpec((1, 1, n_rows), lambda i, be, nv: (i, 0, 0), memory_space=pltpu.SMEM),
                      pl.BlockSpec((1, 1, n_rows), lambda i, be, nv: (jnp.minimum(i + 1, nb - 1), 0, 0),
                                   memory_space=pltpu.SMEM),
                      hbm, hbm, hbm, hbm],
            out_specs=pl.BlockSpec((tm, d), lambda i, be, nv: (i, 0)),
            scratch_shapes=[pltpu.VMEM((n_rows, d), hn.dtype), pltpu.VMEM((tm, d), BF16)]
            + _weight_scratch(d, tf)[:3] + [pltpu.SemaphoreType.DMA(()), pltpu.SemaphoreType.DMA((3, 2))],
        ),
        out_shape=jax.ShapeDtypeStruct((nb * tm, d), F32),
        compiler_params=_params(("arbitrary",)),
        name="ffn_moe",
    )(block_e, n_valid, src3, src3, hn, wg, wu, wd)


def _conv_silu(tail_ref, x_ref, cw_ref, r0, lo, bias=None, act=True):
    sl = slice(lo, lo + LANES)
    win = jnp.concatenate([tail_ref[:, sl], x_ref[pl.ds(r0, CHUNK), sl]], axis=0)
    w = cw_ref[:, sl]
    y = win * w[CONV_K - 1:CONV_K]
    for k in range(1, CONV_K):
        y = y + pltpu.roll(win, k, 0) * w[CONV_K - 1 - k:CONV_K - k]
    y = y[SUBLANES:]
    if bias is not None:
        y = y + bias
    return _silu(y) if act else y


def _rms_gate(o, gain, z):
    ms = jnp.mean(o * o, axis=-1, keepdims=True)
    return o * lax.rsqrt(ms + EPS) * gain * _silu(z)


def _gdn_kernel(q_ref, k_ref, v_ref, z_ref, beta_ref, alpha_ref, cw_ref, alog_ref, dtb_ref, na_ref,
                o_ref, s_ref, tq_ref, tk_ref, tv_ref, *, n_heads):
    C = CHUNK
    tb = q_ref.shape[0]

    @pl.when(pl.program_id(1) == 0)
    def _():
        s_ref[...] = jnp.zeros_like(s_ref)
        tq_ref[...] = jnp.zeros_like(tq_ref)
        tk_ref[...] = jnp.zeros_like(tk_ref)
        tv_ref[...] = jnp.zeros_like(tv_ref)

    row = lax.broadcasted_iota(jnp.int32, (C, C), 0)
    col = lax.broadcasted_iota(jnp.int32, (C, C), 1)
    causal = row >= col
    strict = row > col
    eye = (row == col).astype(F32)
    neg_a = -jnp.exp(alog_ref[...])
    dtb = dtb_ref[...]
    gain = na_ref[...]
    scale = float(LANES) ** -0.5

    heads = range(n_heads)
    units = [(ci, h) for ci in range(2) for h in heads]

    def pair(p, carry):
        r0s = [pl.multiple_of(p * 2 * C, C), pl.multiple_of(p * 2 * C + C, C)]
        low, att, rhs, wqg, kg, eglast = {}, {}, {}, {}, {}, {}
        for ci in range(2):
            r0 = r0s[ci]
            beta_c = jax.nn.sigmoid(beta_ref[pl.ds(r0, C), :])
            g_c = neg_a * _softplus(alpha_ref[pl.ds(r0, C), :] + dtb)
            gam_c = _cumsum_rows(g_c)
            gam_t = gam_c.T
            prev = (tq_ref, tk_ref, tv_ref) if ci == 0 else tuple(
                x.at[pl.ds(r0 - SUBLANES, SUBLANES)] for x in (q_ref, k_ref, v_ref))
            for h in heads:
                lo = h * LANES
                q = _conv_silu(prev[0], q_ref, cw_ref.at[0], r0, lo)
                k = _conv_silu(prev[1], k_ref, cw_ref.at[1], r0, lo)
                v = _conv_silu(prev[2], v_ref, cw_ref.at[2], r0, lo)
                q = q * lax.rsqrt(jnp.sum(q * q, axis=-1, keepdims=True) + EPS) * scale
                k = k * lax.rsqrt(jnp.sum(k * k, axis=-1, keepdims=True) + EPS)
                gcol = gam_c[:, h:h + 1]
                grow = gam_t[h:h + 1, :]
                bcol = beta_c[:, h:h + 1]
                dec = jnp.where(causal, jnp.exp(jnp.where(causal, gcol - grow, 0.0)), 0.0)
                kb = k * bcol
                m1 = _dot_nt(jnp.concatenate([kb, q], axis=0).astype(BF16), k.astype(BF16))
                u = (ci, h)
                low[u] = jnp.where(strict, m1[:C] * dec, 0.0)
                att[u] = (m1[C:] * dec).astype(BF16)
                egam = jnp.exp(gcol)
                glast = gam_c[C - 1:C, h:h + 1]
                rhs[u] = jnp.concatenate([v * bcol, kb * egam], axis=1).astype(BF16)
                wqg[u] = (q * egam).astype(BF16)
                kg[u] = (k * jnp.exp(glast - gcol)).astype(BF16)
                eglast[u] = jnp.exp(glast)
        tinv = {u: eye - low[u] for u in units}
        lp = {u: low[u].astype(BF16) for u in units}
        for it in range(int(math.log2(C)) - 1):
            lp = {u: _dot(lp[u], lp[u]).astype(BF16) for u in units}
            tinv = {u: tinv[u] + _dot(tinv[u].astype(BF16), lp[u]) for u in units}
        sol = {u: _dot(tinv[u].astype(BF16), rhs[u]) for u in units}
        for ci in range(2):
            r0 = r0s[ci]
            s_old = [s_ref[h] for h in heads]
            wq = [_dot(jnp.concatenate([sol[(ci, h)][:, LANES:].astype(BF16), wqg[(ci, h)]], axis=0),
                       s_old[h].astype(BF16)) for h in heads]
            vnb = [(sol[(ci, h)][:, :LANES] - wq[h][:C]).astype(BF16) for h in heads]
            for h in heads:
                s_ref[h] = eglast[(ci, h)] * s_old[h] + _dot_tn(kg[(ci, h)], vnb[h])
            for h in heads:
                lo = h * LANES
                o = wq[h][C:] + _dot(att[(ci, h)], vnb[h])
                z = z_ref[pl.ds(r0, C), lo:lo + LANES]
                o_ref[pl.ds(r0, C), lo:lo + LANES] = _rms_gate(o, gain, z).astype(o_ref.dtype)
        last = r0s[1] + C - SUBLANES
        tq_ref[...] = q_ref[pl.ds(last, SUBLANES), :]
        tk_ref[...] = k_ref[pl.ds(last, SUBLANES), :]
        tv_ref[...] = v_ref[pl.ds(last, SUBLANES), :]
        return carry

    lax.fori_loop(0, tb // (2 * C), pair, 0)


def _gdn(proj, conv_w, a_log, dt_bias, norm_a, bsz, seq, wa, tb=MIXER_TB):
    n_heads = wa // LANES
    nt = seq // tb
    gate_blk = 8 * wa // LANES

    def colblk(cb):
        return pl.BlockSpec((tb, wa), lambda b, t: (b * nt + t, cb))

    def pad_row(p):
        return jnp.zeros((1, LANES), F32).at[0, :n_heads].set(p.astype(F32))

    small = lambda shape: pl.BlockSpec(shape, lambda b, t: (0,) * len(shape))
    return pl.pallas_call(
        functools.partial(_gdn_kernel, n_heads=n_heads),
        grid=(bsz, nt),
        in_specs=[colblk(0), colblk(1), colblk(2), colblk(3),
                  pl.BlockSpec((tb, LANES), lambda b, t: (b * nt + t, gate_blk)),
                  pl.BlockSpec((tb, LANES), lambda b, t: (b * nt + t, gate_blk + 1)),
                  small((3, CONV_K, wa)), small((1, LANES)), small((1, LANES)), small((1, LANES))],
        out_specs=pl.BlockSpec((tb, wa), lambda b, t: (b * nt + t, 0)),
        out_shape=jax.ShapeDtypeStruct((bsz * seq, wa), BF16),
        scratch_shapes=[pltpu.VMEM((n_heads, LANES, LANES), F32),
                        pltpu.VMEM((SUBLANES, wa), F32), pltpu.VMEM((SUBLANES, wa), F32),
                        pltpu.VMEM((SUBLANES, wa), F32)],
        compiler_params=_params(("parallel", "arbitrary")),
        name="gdn",
    )(proj, proj, proj, proj, proj, proj,
      conv_w.astype(F32).reshape(CONV_K, 3, wa).transpose(1, 0, 2),
      pad_row(a_log), pad_row(dt_bias), norm_a.astype(F32).reshape(1, LANES))


def _ret_kernel(q_ref, k_ref, v_ref, g_ref, cos_ref, sin_ref, nb_ref, o_ref, r_ref, dm_ref, xi_ref, zeta_ref,
                *, n_heads):
    C = CHUNK
    tb = q_ref.shape[0]
    log_gammas = [math.log1p(-(2.0 ** (-5.0 - h))) for h in range(n_heads)]

    @pl.when(pl.program_id(1) == 0)
    def _():
        r_ref[...] = jnp.zeros_like(r_ref)
        row = lax.broadcasted_iota(jnp.int32, (C, C), 0)
        col = lax.broadcasted_iota(jnp.int32, (C, C), 1)
        rel = (row - col).astype(F32)
        idx = lax.broadcasted_iota(jnp.int32, (C, LANES), 0).astype(F32)
        for h in range(n_heads):
            dm_ref[h] = jnp.where(rel >= 0, jnp.exp(jnp.maximum(rel, 0.0) * log_gammas[h]), 0.0)
            xi_ref[h] = jnp.exp((idx + 1.0) * log_gammas[h])
            zeta_ref[h] = jnp.exp((C - 1.0 - idx) * log_gammas[h])

    scale = float(LANES) ** -0.5

    def chunk(c, carry):
        r0 = pl.multiple_of(c * C, C)
        cos = cos_ref[pl.ds(r0, C), :]
        sin = sin_ref[pl.ds(r0, C), :]
        heads = range(n_heads)
        qs, ks, vbs = [], [], []
        for h in heads:
            sl = slice(h * LANES, (h + 1) * LANES)
            q = q_ref[pl.ds(r0, C), sl]
            k = k_ref[pl.ds(r0, C), sl]
            qs.append(q * cos + pltpu.roll(q, LANES // 2, 1) * sin)
            ks.append((k * cos + pltpu.roll(k, LANES // 2, 1) * sin) * scale)
            vbs.append(v_ref[pl.ds(r0, C), sl].astype(BF16))
        scores = [(_dot_nt(qs[h].astype(BF16), ks[h].astype(BF16)) * dm_ref[h]).astype(BF16) for h in heads]
        r_old = [r_ref[h] for h in heads]
        outs = [_dot(scores[h], vbs[h]) + _dot((qs[h] * xi_ref[h]).astype(BF16), r_old[h].astype(BF16))
                for h in heads]
        for h in heads:
            r_ref[h] = math.exp(C * log_gammas[h]) * r_old[h] + _dot_tn((ks[h] * zeta_ref[h]).astype(BF16), vbs[h])
        for h in heads:
            sl = slice(h * LANES, (h + 1) * LANES)
            o = outs[h]
            mu = jnp.mean(o, axis=-1, keepdims=True)
            oc = o - mu
            var = jnp.mean(oc * oc, axis=-1, keepdims=True)
            g = g_ref[pl.ds(r0, C), sl]
            o_ref[pl.ds(r0, C), sl] = (oc * lax.rsqrt(var + EPS) * nb_ref[:, sl] * _silu(g)).astype(o_ref.dtype)
        return carry

    lax.fori_loop(0, tb // C, chunk, 0)


def _retention(proj, norm_b, bsz, seq, wb, col0, tb=MIXER_TB):
    n_heads = wb // LANES
    nt = seq // tb
    half = LANES // 2
    pos = jnp.arange(seq, dtype=F32)
    theta = 1.0 / ROPE_BASE ** jnp.linspace(0.0, 1.0, half, dtype=F32)
    ang = pos[:, None] * theta[None, :]
    cos, sin = jnp.cos(ang), jnp.sin(ang)
    cos2 = jnp.concatenate([cos, cos], axis=-1)
    sin2 = jnp.concatenate([-sin, sin], axis=-1)

    def colblk(cb):
        return pl.BlockSpec((tb, wb), lambda b, t: (b * nt + t, col0 + cb))

    return pl.pallas_call(
        functools.partial(_ret_kernel, n_heads=n_heads),
        grid=(bsz, nt),
        in_specs=[colblk(0), colblk(1), colblk(2), colblk(3),
                  pl.BlockSpec((tb, LANES), lambda b, t: (t, 0)),
                  pl.BlockSpec((tb, LANES), lambda b, t: (t, 0)),
                  pl.BlockSpec((1, wb), lambda b, t: (0, 0))],
        out_specs=pl.BlockSpec((tb, wb), lambda b, t: (b * nt + t, 0)),
        out_shape=jax.ShapeDtypeStruct((bsz * seq, wb), BF16),
        scratch_shapes=[pltpu.VMEM((n_heads, LANES, LANES), F32), pltpu.VMEM((n_heads, CHUNK, CHUNK), F32),
                        pltpu.VMEM((n_heads, CHUNK, LANES), F32), pltpu.VMEM((n_heads, CHUNK, LANES), F32)],
        compiler_params=_params(("parallel", "arbitrary")),
        name="retention",
    )(proj, proj, proj, proj, cos2, sin2, norm_b.astype(F32).reshape(1, wb))


def _rglru_kernel(y_ref, x_ref, cw_ref, cb_ref, wa_ref, ba_ref, wx_ref, bx_ref, lam_ref,
                  o_ref, tail_ref, h_ref, *, n_groups):
    C = CHUNK
    tb = x_ref.shape[0]

    @pl.when(pl.program_id(1) == 0)
    def _():
        tail_ref[...] = jnp.zeros_like(tail_ref)
        h_ref[...] = jnp.zeros_like(h_ref)

    rows = lax.broadcasted_iota(jnp.int32, (C, LANES), 0)

    def chunk(c, carry):
        r0 = pl.multiple_of(c * C, C)
        for g in range(n_groups):
            lo = g * LANES
            sl = slice(lo, lo + LANES)
            u = _conv_silu(tail_ref, x_ref, cw_ref, r0, lo, bias=cb_ref[:, sl], act=False)
            ub = u.astype(BF16)
            r_gate = jax.nn.sigmoid(_dot(ub, wa_ref[g]) + ba_ref[:, sl])
            i_gate = jax.nn.sigmoid(_dot(ub, wx_ref[g]) + bx_ref[:, sl])
            log_a = -RG_C * r_gate * _softplus(-lam_ref[:, sl])
            a = jnp.exp(log_a)
            th = jnp.tanh(log_a)
            b = jnp.sqrt(-2.0 * th / (1.0 - th)) * (i_gate * u)
            d = 1
            while d < C:
                keep = rows >= d
                b = b + a * jnp.where(keep, pltpu.roll(b, d, 0), 0.0)
                a = a * jnp.where(keep, pltpu.roll(a, d, 0), 1.0)
                d *= 2
            h = b + a * h_ref[:, sl]
            h_ref[:, sl] = h[C - 1:C]
            y = y_ref[pl.ds(r0, C), sl]
            o_ref[pl.ds(r0, C), sl] = (jax.nn.gelu(y, approximate=True) * h).astype(o_ref.dtype)
        tail_ref[...] = x_ref[pl.ds(r0 + C - SUBLANES, SUBLANES), :]
        return carry

    lax.fori_loop(0, tb // C, chunk, 0)


def _rglru(proj, conv_w, conv_b, rg_wa, rg_ba, rg_wx, rg_bx, rg_lambda, bsz, seq, wc, tb=MIXER_TB):
    n_groups = wc // LANES
    nt = seq // tb
    row = lambda p: p.astype(F32).reshape(1, wc)
    small = lambda shape: pl.BlockSpec(shape, lambda b, t: (0,) * len(shape))
    return pl.pallas_call(
        functools.partial(_rglru_kernel, n_groups=n_groups),
        grid=(bsz, nt),
        in_specs=[pl.BlockSpec((tb, wc), lambda b, t: (b * nt + t, 0)),
                  pl.BlockSpec((tb, wc), lambda b, t: (b * nt + t, 1)),
                  small((CONV_K, wc)), small((1, wc)),
                  small((n_groups, LANES, LANES)), small((1, wc)),
                  small((n_groups, LANES, LANES)), small((1, wc)), small((1, wc))],
        out_specs=pl.BlockSpec((tb, wc), lambda b, t: (b * nt + t, 0)),
        out_shape=jax.ShapeDtypeStruct((bsz * seq, wc), BF16),
        scratch_shapes=[pltpu.VMEM((SUBLANES, wc), F32), pltpu.VMEM((1, wc), F32)],
        compiler_params=_params(("parallel", "arbitrary")),
        name="rglru",
    )(proj, proj, conv_w.astype(F32), row(conv_b), rg_wa.astype(BF16), row(rg_ba),
      rg_wx.astype(BF16), row(rg_bx), row(rg_lambda))


def _hgrn2_kernel(q_ref, f_ref, i_ref, g_ref, lb_ref, nd_ref, o_ref, st_ref, gam_ref, mask_ref, *, n_heads):
    C = CHUNK
    tb = q_ref.shape[0]
    levels = [C >> (l + 1) for l in range(int(math.log2(C)))]

    @pl.when(pl.program_id(1) == 0)
    def _():
        st_ref[...] = jnp.zeros_like(st_ref)
        row = lax.broadcasted_iota(jnp.int32, (C, C), 0)
        col = lax.broadcasted_iota(jnp.int32, (C, C), 1)
        mask_ref[0] = (row == col).astype(F32)
        for l, b in enumerate(levels):
            pair = ((row & -(2 * b)) == (col & -(2 * b))) & ((row & b) != 0) & ((col & b) == 0)
            mask_ref[l + 1] = pair.astype(F32)

    rows = lax.broadcasted_iota(jnp.int32, (C, LANES), 0)
    gain = nd_ref[...]

    def chunk(c, carry):
        r0 = pl.multiple_of(c * C, C)
        for h in range(n_heads):
            lo = h * LANES
            sl = slice(lo, lo + LANES)
            lb = lb_ref[:, sl]
            f_lin = f_ref[pl.ds(r0, C), sl]
            log_sig = jnp.minimum(f_lin, 0.0) - jnp.log(1.0 + jnp.exp(-jnp.abs(f_lin)))
            la, lc = jnp.log(lb), jnp.log1p(-lb) + log_sig
            log_f = jnp.maximum(la, lc) + jnp.log(1.0 + jnp.exp(-jnp.abs(la - lc)))
            k = (1.0 - lb) * jax.nn.sigmoid(-f_lin)
            q = _silu(q_ref[pl.ds(r0, C), sl])
            v = i_ref[pl.ds(r0, C), sl]
            vb = v.astype(BF16)
            gam = _cumsum_rows(log_f)
            gam_ref[...] = gam
            att = mask_ref[0] * _dot_nt(q.astype(BF16), k.astype(BF16))
            for l, b in enumerate(levels):
                nblk = C // (2 * b)
                if b >= SUBLANES // 2:
                    parts = [jnp.broadcast_to(gam_ref[pl.ds(j * 2 * b + b - 1, 1), :], (2 * b, LANES)) for j in range(nblk)]
                    gmid = jnp.concatenate(parts, axis=0) if nblk > 1 else parts[0]
                else:
                    off = (rows & (2 * b - 1)) - (b - 1)
                    gmid = jnp.zeros_like(gam)
                    for o_ in range(-(b - 1), b + 1):
                        gmid = jnp.where(off == o_, pltpu.roll(gam, o_ % C, 0), gmid)
                t = (jnp.where((rows & b) != 0, q, k) * jnp.exp(-jnp.abs(gam - gmid))).astype(BF16)
                att = att + mask_ref[l + 1] * _dot_nt(t, t)
            st = st_ref[h]
            o = _dot(att.astype(BF16), vb) + _dot_nt((q * jnp.exp(gam)).astype(BF16), st.astype(BF16))
            glast = gam[C - 1:C, :]
            kg = (k * jnp.exp(glast - gam)).astype(BF16)
            st_ref[h] = st * jnp.exp(glast) + _dot_tn(vb, kg)
            g = g_ref[pl.ds(r0, C), sl]
            o_ref[pl.ds(r0, C), sl] = _rms_gate(o, gain, g).astype(o_ref.dtype)
        return carry

    lax.fori_loop(0, tb // C, chunk, 0)


def _hgrn2(proj, lb, norm_d, bsz, seq, wd, col0, tb=MIXER_TB):
    n_heads = wd // LANES
    nt = seq // tb

    def colblk(cb):
        return pl.BlockSpec((tb, wd), lambda b, t: (b * nt + t, col0 + cb))

    return pl.pallas_call(
        functools.partial(_hgrn2_kernel, n_heads=n_heads),
        grid=(bsz, nt),
        in_specs=[colblk(0), colblk(1), colblk(2), colblk(3),
                  pl.BlockSpec((1, wd), lambda b, t: (0, 0)), pl.BlockSpec((1, LANES), lambda b, t: (0, 0))],
        out_specs=pl.BlockSpec((tb, wd), lambda b, t: (b * nt + t, 0)),
        out_shape=jax.ShapeDtypeStruct((bsz * seq, wd), BF16),
        scratch_shapes=[pltpu.VMEM((n_heads, LANES, LANES), F32), pltpu.VMEM((CHUNK, LANES), F32),
                        pltpu.VMEM((int(math.log2(CHUNK)) + 1, CHUNK, CHUNK), F32)],
        compiler_params=_params(("parallel", "arbitrary")),
        name="hgrn2",
    )(proj, proj, proj, proj, lb.astype(F32).reshape(1, wd), norm_d.astype(F32).reshape(1, LANES))


def _router_kernel(x_ref, g_ref, wr_ref, hn_ref, r_ref):
    x = x_ref[...]
    ms = jnp.mean(x * x, axis=-1, keepdims=True)
    hn = x * lax.rsqrt(ms + EPS) * g_ref[...]
    hn_ref[...] = hn
    w = wr_ref[...]
    h1 = hn.astype(BF16)
    h2 = (hn - h1.astype(F32)).astype(BF16)
    w1 = w.astype(BF16)
    w2 = (w - w1.astype(F32)).astype(BF16)
    logits = _dot(h1, w1) + (_dot(h1, w2) + _dot(h2, w1))
    lane = lax.broadcasted_iota(jnp.int32, logits.shape, 1)
    lg = jnp.where(lane < N_EXPERTS, logits, -jnp.inf)
    m1 = jnp.max(lg, axis=-1, keepdims=True)
    i1 = jnp.min(jnp.where(lg == m1, lane, LANES), axis=-1, keepdims=True)
    lg2 = jnp.where(lane == i1, -jnp.inf, lg)
    m2 = jnp.max(lg2, axis=-1, keepdims=True)
    i2 = jnp.min(jnp.where(lg2 == m2, lane, LANES), axis=-1, keepdims=True)
    e = jnp.exp(m2 - m1)
    g1 = 1.0 / (1.0 + e)
    g2 = e / (1.0 + e)
    r_ref[...] = jnp.where(lane == 0, i1.astype(F32), jnp.where(lane == 1, i2.astype(F32),
                           jnp.where(lane == 2, g1, jnp.where(lane == 3, g2, 0.0))))


def _router(x, g, w_router, tm=256):
    t, d = x.shape
    wr = jnp.zeros((d, LANES), F32).at[:, :N_EXPERTS].set(w_router.astype(F32))
    return pl.pallas_call(
        _router_kernel,
        grid=(t // tm,),
        in_specs=[pl.BlockSpec((tm, d), lambda i: (i, 0)), pl.BlockSpec((1, d), lambda i: (0, 0)),
                  pl.BlockSpec((d, LANES), lambda i: (0, 0))],
        out_specs=[pl.BlockSpec((tm, d), lambda i: (i, 0)), pl.BlockSpec((tm, LANES), lambda i: (i, 0))],
        out_shape=[jax.ShapeDtypeStruct((t, d), F32), jax.ShapeDtypeStruct((t, LANES), F32)],
        compiler_params=_params(("parallel",)),
        name="router",
    )(x, g.reshape(1, d), wr)


def _combine_kernel(dest_ref, dest_next_ref, r_ref, x_ref, y_ref, gf_ref, o_ref, buf_ref, sems):
    i = pl.program_id(0)
    tb = x_ref.shape[0]
    slot = lax.rem(i, 2)

    @pl.when(i == 0)
    def _():
        _gather_rows(dest_ref, TOP_K * tb, y_ref, buf_ref.at[0], sems.at[0])

    @pl.when(i + 1 < pl.num_programs(0))
    def _():
        _gather_rows(dest_next_ref, TOP_K * tb, y_ref, buf_ref.at[1 - slot], sems.at[1 - slot])

    _wait_rows(TOP_K * tb, y_ref, buf_ref.at[slot], sems.at[slot])
    gates = r_ref[...]
    acc = x_ref[...] + gates[:, 2:3] * buf_ref[slot, :tb] + gates[:, 3:4] * buf_ref[slot, tb:]
    ms = jnp.mean(acc * acc, axis=-1, keepdims=True)
    o_ref[...] = acc * lax.rsqrt(ms + EPS) * gf_ref[...]


def _combine(x, y_pad, dest, route, g_final, tb=256):
    t, d = x.shape
    nb = t // tb
    table = dest.reshape(nb, tb, TOP_K).transpose(0, 2, 1).reshape(nb, 1, TOP_K * tb)
    return pl.pallas_call(
        _combine_kernel,
        grid=(nb,),
        in_specs=[pl.BlockSpec((1, 1, TOP_K * tb), lambda i: (i, 0, 0), memory_space=pltpu.SMEM),
                  pl.BlockSpec((1, 1, TOP_K * tb), lambda i: (jnp.minimum(i + 1, nb - 1), 0, 0),
                               memory_space=pltpu.SMEM),
                  pl.BlockSpec((tb, LANES), lambda i: (i, 0)),
                  pl.BlockSpec((tb, d), lambda i: (i, 0)),
                  pl.BlockSpec(memory_space=pl.ANY),
                  pl.BlockSpec((1, d), lambda i: (0, 0))],
        out_specs=pl.BlockSpec((tb, d), lambda i: (i, 0)),
        out_shape=jax.ShapeDtypeStruct((t, d), F32),
        scratch_shapes=[pltpu.VMEM((2, TOP_K * tb, d), F32), pltpu.SemaphoreType.DMA((2,))],
        compiler_params=_params(("arbitrary",)),
        name="moe_combine",
    )(table, table, route, x, y_pad, g_final.reshape(1, d))


def _routing_tables(route, tm):
    t = route.shape[0]
    assert (t * TOP_K) % tm == 0
    flat_e = route[:, :TOP_K].astype(jnp.int32).reshape(-1)
    onehot = (flat_e[:, None] == jnp.arange(N_EXPERTS, dtype=jnp.int32)[None, :]).astype(jnp.int32)
    cum = jnp.cumsum(onehot, axis=0)
    rank = jnp.sum((cum - onehot) * onehot, axis=1)
    counts = cum[-1]
    padded = (counts + tm - 1) // tm * tm
    p_end = jnp.cumsum(padded)
    p_start = p_end - padded
    dest = (jnp.sum(onehot * p_start[None, :], axis=1) + rank).astype(jnp.int32)
    n_blocks = -(-(t * TOP_K) // tm) + N_EXPERTS
    block_e = jnp.minimum(jnp.searchsorted(p_end, jnp.arange(n_blocks, dtype=jnp.int32) * tm, side='right'),
                          N_EXPERTS - 1).astype(jnp.int32)
    n_valid = (p_end[-1] // tm).astype(jnp.int32).reshape(1)
    order = jnp.argsort(flat_e, stable=True).astype(jnp.int32)
    rows = jnp.arange(n_blocks * tm, dtype=jnp.int32)
    row_e = jnp.repeat(block_e, tm)
    row_rank = rows - p_start[row_e]
    entry = order[jnp.clip((jnp.cumsum(counts) - counts)[row_e] + row_rank, 0, t * TOP_K - 1)]
    src = jnp.where(row_rank < counts[row_e], entry // TOP_K, 0).astype(jnp.int32)
    return dest, src, block_e, n_valid


def kernel(x, norm_mix, norm_ffn, w_in_ab, conv_a, a_log, dt_bias, norm_a, norm_b, w_out_ab, w_gate_dense,
           w_up_dense, w_down_dense, w_in_cd, conv_c_w, conv_c_b, rg_wa, rg_ba, rg_wx, rg_bx, rg_lambda, hgrn_lb,
           norm_d, w_out_cd, w_router, w_gate_moe, w_up_moe, w_down_moe, norm_final):
    bsz, seq, d = x.shape
    xf = x.reshape(bsz * seq, d).astype(F32)
    x1 = _mixer_ab(xf, norm_mix[0], w_in_ab[0], conv_a[0], a_log[0], dt_bias[0], norm_a[0], norm_b[0], w_out_ab[0],
                   bsz, seq)
    x2 = _dense_ffn(x1, norm_ffn[0], w_gate_dense.astype(BF16), w_up_dense.astype(BF16), w_down_dense.astype(BF16))
    lb_soft = jax.nn.softmax(hgrn_lb.astype(F32), axis=0)
    lb_all = jnp.cumsum(lb_soft, axis=0) - lb_soft[0:1]
    x3 = _mixer_cd(x2, norm_mix[1], w_in_cd[0], conv_c_w[0], conv_c_b[0], rg_wa[0], rg_ba[0], rg_wx[0], rg_bx[0],
                   rg_lambda[0], lb_all[1], norm_d[0], w_out_cd[0], bsz, seq)
    out = _moe_ffn_final(x3, norm_ffn[1], w_router[0], w_gate_moe[0], w_up_moe[0], w_down_moe[0], norm_final)
    return out.reshape(bsz, seq, d).astype(x.dtype)


def _mixer_ab(xf, g_mix, w_in, conv_a, a_log, dt_bias, norm_a, norm_b, w_out, bsz, seq):
    wa = xf.shape[1] // 2
    n_ha = wa // LANES
    pad = lambda cols: jnp.pad(cols, ((0, 0), (0, LANES - cols.shape[1])))
    w_perm = jnp.concatenate([w_in[:, :4 * wa], w_in[:, 4 * wa + 2 * n_ha:],
                              pad(w_in[:, 4 * wa:4 * wa + n_ha]), pad(w_in[:, 4 * wa + n_ha:4 * wa + 2 * n_ha])],
                             axis=1).astype(BF16)
    proj = _rms_matmul(xf, g_mix, w_perm, tm=512, tn=2816, name="in_proj_ab")
    o_a = _gdn(proj, conv_a, a_log, dt_bias, norm_a, bsz, seq, wa)
    o_b = _retention(proj, norm_b, bsz, seq, wa, col0=4)
    wo = w_out.astype(BF16)
    return _matmul([o_a, o_b], [wo[:wa], wo[wa:]], res=xf, tm=512, tn=2 * wa, name="out_proj_ab")


def _mixer_cd(x2, g_mix, w_in, conv_w, conv_b, rg_wa, rg_ba, rg_wx, rg_bx, rg_lambda, lb, norm_d, w_out, bsz, seq):
    wc = x2.shape[1] // 2
    proj = _rms_matmul(x2, g_mix, w_in.astype(BF16), tn=1536, name="in_proj_cd")
    o_c = _rglru(proj, conv_w, conv_b, rg_wa, rg_ba, rg_wx, rg_bx, rg_lambda, bsz, seq, wc)
    o_d = _hgrn2(proj, lb, norm_d, bsz, seq, wc, col0=2)
    wo = w_out.astype(BF16)
    return _matmul([o_c, o_d], [wo[:wc], wo[wc:]], res=x2, tm=512, tn=2 * wc, name="out_proj_cd")


def _moe_ffn_final(x3, g_ffn, w_router, w_gate, w_up, w_down, g_final, tm=1024):
    hn_moe, route = _router(x3, g_ffn, w_router)
    dest, src, block_e, n_valid = _routing_tables(route, tm)
    y_pad = _moe_ffn_call(hn_moe, src, w_gate.astype(BF16), w_up.astype(BF16), w_down.astype(BF16), block_e, n_valid,
                          tm=tm)
    return _combine(x3, y_pad, dest, route, g_final)
```
